```python
import math
import jax
import jax.numpy as jnp
from jax import lax
import numpy as np

D_MODEL = 2048
BATCH = 2
SEQ = 4096
DEPTH = 4
DEC_BATCH = 8
DEC_SEQ = 4
PAST_LEN = 16384
PAGE_SIZE = 128

HEAD_DIM = 64
N_HEADS = D_MODEL // HEAD_DIM
N_KV_HEADS = N_HEADS // 4
ROPE_DIM = HEAD_DIM // 4
ROPE_THETA = 500000.0
WINDOWS = (128, 512, 2048)
DILATIONS = (1, 4, 16)
N_DIL_GROUPS = len(WINDOWS)
Q_COLS = N_DIL_GROUPS * N_HEADS * HEAD_DIM
KV_COLS = N_DIL_GROUPS * N_KV_HEADS * HEAD_DIM
QKV_DIM = Q_COLS + 2 * KV_COLS

SSM_EXPAND = 2
D_INNER = SSM_EXPAND * D_MODEL
SSM_HEAD_DIM = 64
SSM_HEADS = D_INNER // SSM_HEAD_DIM
SSM_GROUPS = 8
HEADS_PER_GROUP = SSM_HEADS // SSM_GROUPS
D_STATE = 128
CONV_W = 4
BC_DIM = SSM_GROUPS * D_STATE
CONV_DIM = D_INNER + 2 * BC_DIM
IN_DIM = D_INNER + CONV_DIM + SSM_HEADS
SSM_CHUNK = 128

D_FF = 4 * D_MODEL
N_EXPERTS = 8
TOP_K = 2
D_FF_EXPERT = 7 * D_MODEL // 2

N_ATTN_LAYERS = (DEPTH + 1) // 2
N_SSM_LAYERS = DEPTH // 2
N_DENSE_LAYERS = (DEPTH + 1) // 2
N_MOE_LAYERS = DEPTH // 2
EPS = 1e-6

kernel_name = 'hybrid_dilated_attn_ssd_moe_step'


def rmsnorm(x, g):
    xf = x.astype(jnp.float32)
    y = xf * lax.rsqrt(jnp.mean(xf * xf, axis=-1, keepdims=True) + EPS)
    return (y * g.astype(jnp.float32)).astype(x.dtype)


def rope_partial(x, pos):
    half = ROPE_DIM // 2
    inv_freq = jnp.exp(-math.log(ROPE_THETA) * jnp.arange(half, dtype=jnp.float32) / half)
    ang = pos.astype(jnp.float32)[:, None] * inv_freq[None, :]
    bshape = (ang.shape[0],) + (1,) * (x.ndim - 3) + (half,)
    cos = jnp.cos(ang).reshape(bshape)
    sin = jnp.sin(ang).reshape(bshape)
    xr = x[..., :ROPE_DIM].astype(jnp.float32)
    x1, x2 = xr[..., :half], xr[..., half:]
    rot = jnp.concatenate([x1 * cos - x2 * sin, x2 * cos + x1 * sin], axis=-1)
    return jnp.concatenate([rot.astype(x.dtype), x[..., ROPE_DIM:]], axis=-1)


def attn_project(h, w_qkv, q_gain, k_gain, pos):
    B, S, _ = h.shape
    p = h @ w_qkv
    q = p[..., :Q_COLS].reshape(B, S, N_DIL_GROUPS, N_HEADS, HEAD_DIM)
    k = p[..., Q_COLS:Q_COLS + KV_COLS].reshape(B, S, N_DIL_GROUPS, N_KV_HEADS, HEAD_DIM)
    v = p[..., Q_COLS + KV_COLS:].reshape(B, S, N_DIL_GROUPS, N_KV_HEADS, HEAD_DIM)
    q = rope_partial(rmsnorm(q, q_gain), pos)
    k = rope_partial(rmsnorm(k, k_gain), pos)
    return q, k, v


def dilated_band_attention(q, k, v, window, dil):
    B, S, H, Dh = q.shape
    KV = k.shape[2]
    G = H // KV
    qb = window // dil
    L = S // dil
    nb = -(-L // qb)
    Lp = nb * qb

    def to_res(t):
        t = t.reshape(B, L, dil, t.shape[2], Dh).transpose(0, 2, 1, 3, 4)
        return jnp.pad(t, ((0, 0), (0, 0), (0, Lp - L), (0, 0), (0, 0)))

    qr = to_res(q).reshape(B, dil, nb, qb, KV, G, Dh)
    kr = to_res(k).reshape(B, dil, nb, qb, KV, Dh)
    vr = to_res(v).reshape(B, dil, nb, qb, KV, Dh)

    def with_prev(t):
        prev = jnp.concatenate([jnp.zeros_like(t[:, :, :1]), t[:, :, :-1]], axis=2)
        return jnp.concatenate([prev, t], axis=3)

    kb = with_prev(kr)
    vb = with_prev(vr)
    s = jnp.einsum('brnqkgd,brnskd->brnkgqs', qr, kb).astype(jnp.float32) * (Dh ** -0.5)
    qi = jnp.arange(qb)[:, None]
    si = jnp.arange(2 * qb)[None, :]
    band = (si >= qi) & (si <= qi + qb)
    blk = jnp.arange(nb)[:, None, None]
    valid = band[None] & ((blk > 0) | (si >= qb)[None])
    s = jnp.where(valid[None, None, :, None, None], s, -jnp.inf)
    lse = jax.nn.logsumexp(s, axis=-1)
    p = jnp.exp(s - lse[..., None])
    o = jnp.einsum('brnkgqs,brnskd->brnqkgd', p.astype(v.dtype), vb)
    o = o.reshape(B, dil, Lp, H, Dh)[:, :, :L].transpose(0, 2, 1, 3, 4).reshape(B, S, H, Dh)
    lse = lse.transpose(0, 1, 2, 5, 3, 4).reshape(B, dil, Lp, H)[:, :, :L]
    lse = lse.transpose(0, 2, 1, 3).reshape(B, S, H)
    return o, lse


def dilated_gather_attention(q, k_new, v_new, kv_buf, window, dil):
    B, T, H, Dh = q.shape
    KV = k_new.shape[2]
    G = H // KV
    Wb = kv_buf.shape[1]
    kc = jnp.concatenate([kv_buf[:, :, 0].astype(k_new.dtype), k_new], axis=1)
    vc = jnp.concatenate([kv_buf[:, :, 1].astype(v_new.dtype), v_new], axis=1)
    nk = window // dil + 1
    idx = Wb + jnp.arange(T)[:, None] - dil * jnp.arange(nk)[None, :]
    valid = idx >= 0
    idx = jnp.maximum(idx, 0)
    kg = jnp.take(kc, idx, axis=1)
    vg = jnp.take(vc, idx, axis=1)
    qg = q.reshape(B, T, KV, G, Dh)
    s = jnp.einsum('btkgd,btjkd->btkgj', qg, kg).astype(jnp.float32) * (Dh ** -0.5)
    s = jnp.where(valid[None, :, None, None, :], s, -jnp.inf)
    lse = jax.nn.logsumexp(s, axis=-1)
    p = jnp.exp(s - lse[..., None])
    o = jnp.einsum('btkgj,btjkd->btkgd', p.astype(vg.dtype), vg).reshape(B, T, H, Dh)
    new_buf = jnp.stack([kc[:, -Wb:], vc[:, -Wb:]], axis=2)
    return o, lse.reshape(B, T, H), new_buf


def merge_groups(outs, lses, w_o):
    o = jnp.stack(outs, 0).astype(jnp.float32)
    wts = jax.nn.softmax(jnp.stack(lses, 0), axis=0)
    o = jnp.einsum('gbsh,gbshd->bshd', wts, o)
    B, S = o.shape[:2]
    return o.reshape(B, S, N_HEADS * HEAD_DIM).astype(w_o.dtype) @ w_o


def attn_prompt(h, w_qkv, q_gain, k_gain, w_o):
    B, S, _ = h.shape
    q, k, v = attn_project(h, w_qkv, q_gain, k_gain, jnp.arange(S))
    outs, lses, bufs = [], [], []
    for g in range(N_DIL_GROUPS):
        o, l = dilated_band_attention(q[:, :, g], k[:, :, g], v[:, :, g], WINDOWS[g], DILATIONS[g])
        outs.append(o)
        lses.append(l)
        keep = min(WINDOWS[g], S)
        bufs.append(jnp.stack([k[:, S - keep:, g], v[:, S - keep:, g]], axis=2))
    return merge_groups(outs, lses, w_o), bufs


def attn_sample(h, kv_bufs, w_qkv, q_gain, k_gain, w_o):
    B, T, _ = h.shape
    q, k, v = attn_project(h, w_qkv, q_gain, k_gain, PAST_LEN + jnp.arange(T))
    outs, lses, bufs = [], [], []
    for g in range(N_DIL_GROUPS):
        o, l, nb = dilated_gather_attention(q[:, :, g], k[:, :, g], v[:, :, g], kv_bufs[g],
                                            WINDOWS[g], DILATIONS[g])
        outs.append(o)
        lses.append(l)
        bufs.append(nb)
    return merge_groups(outs, lses, w_o), bufs


def ssd_scan(x, dt, A, Bm, Cm, h0, chunk):
    Bsz, S = x.shape[:2]
    nc = S // chunk

    def chunks(t):
        return t.reshape((Bsz, nc, chunk) + t.shape[2:])

    xc = chunks(x.astype(jnp.float32) * dt[..., None])
    Bc = chunks(Bm.astype(jnp.float32))
    Cc = chunks(Cm.astype(jnp.float32))
    a_cum = jnp.cumsum(chunks(dt * A), axis=2)
    seg = a_cum[:, :, :, None] - a_cum[:, :, None, :]
    causal = jnp.tril(jnp.ones((chunk, chunk), dtype=bool))[:, :, None, None]
    decay = jnp.exp(jnp.where(causal, seg, -jnp.inf))
    cb = jnp.einsum('bcign,bcjgn->bcgij', Cc, Bc)
    y_intra = jnp.einsum('bcgij,bcijgh,bcjghp->bcighp', cb, decay, xc)
    dec_end = jnp.exp(a_cum[:, :, -1:] - a_cum)
    states = jnp.einsum('bcjgn,bcjgh,bcjghp->bcghpn', Bc, dec_end, xc)
    chunk_decay = jnp.exp(a_cum[:, :, -1])

    def step(h_prev, inp):
        st, cd = inp
        return cd[..., None, None] * h_prev + st, h_prev

    h_final, h_enter = lax.scan(step, h0.astype(jnp.float32),
                                (states.transpose(1, 0, 2, 3, 4, 5), chunk_decay.transpose(1, 0, 2, 3)))
    h_enter = h_enter.transpose(1, 0, 2, 3, 4, 5)
    y_inter = jnp.einsum('bcign,bcghpn,bcigh->bcighp', Cc, h_enter, jnp.exp(a_cum))
    y = (y_intra + y_inter).reshape(x.shape)
    return y, h_final


def ssd_mixer(h, conv_prev, ssm_prev, w_in, conv_w, conv_b, dt_bias, a_log, d_skip, norm_w, w_out, chunk):
    Bsz, S, _ = h.shape
    p = h @ w_in
    z = p[..., :D_INNER]
    xbc = p[..., D_INNER:D_INNER + CONV_DIM]
    dt_raw = p[..., D_INNER + CONV_DIM:]
    xpad = jnp.concatenate([conv_prev.astype(xbc.dtype), xbc], axis=1)
    conv = conv_b
    for j in range(CONV_W):
        conv = conv + xpad[:, j:j + S] * conv_w[j]
    conv = jax.nn.silu(conv)
    new_conv = xpad[:, -(CONV_W - 1):]
    xs = conv[..., :D_INNER].reshape(Bsz, S, SSM_GROUPS, HEADS_PER_GROUP, SSM_HEAD_DIM)
    Bm = conv[..., D_INNER:D_INNER + BC_DIM].reshape(Bsz, S, SSM_GROUPS, D_STATE)
    Cm = conv[..., D_INNER + BC_DIM:].reshape(Bsz, S, SSM_GROUPS, D_STATE)
    dt = jax.nn.softplus(dt_raw.astype(jnp.float32) + dt_bias.astype(jnp.float32))
    dt = dt.reshape(Bsz, S, SSM_GROUPS, HEADS_PER_GROUP)
    A = -jnp.exp(a_log.astype(jnp.float32)).reshape(SSM_GROUPS, HEADS_PER_GROUP)
    h0 = ssm_prev.reshape(Bsz, SSM_GROUPS, HEADS_PER_GROUP, SSM_HEAD_DIM, D_STATE)
    y, h_final = ssd_scan(xs, dt, A, Bm, Cm, h0, chunk)
    y = y + d_skip.astype(jnp.float32).reshape(SSM_GROUPS, HEADS_PER_GROUP)[..., None] * xs.astype(jnp.float32)
    g = y.reshape(Bsz, S, D_INNER) * jax.nn.silu(z.astype(jnp.float32))
    g = g.reshape(Bsz, S, SSM_GROUPS, D_INNER // SSM_GROUPS)
    g = g * lax.rsqrt(jnp.mean(g * g, axis=-1, keepdims=True) + EPS)
    g = g.reshape(Bsz, S, D_INNER) * norm_w.astype(jnp.float32)
    out = g.astype(h.dtype) @ w_out
    return out, new_conv, h_final.reshape(Bsz, SSM_HEADS, SSM_HEAD_DIM, D_STATE)


def swiglu(h, w_gate, w_up, w_down):
    return (jax.nn.silu(h @ w_gate) * (h @ w_up)) @ w_down


def moe_swiglu(h, w_router, w_gate, w_up, w_down):
    logits = (h @ w_router).astype(jnp.float32)
    top_v, top_i = lax.top_k(logits, TOP_K)
    gates = jax.nn.softmax(top_v, axis=-1)
    dense_gate = jnp.sum(jax.nn.one_hot(top_i, N_EXPERTS, dtype=jnp.float32) * gates[..., None], axis=-2)
    out = jnp.zeros(h.shape, jnp.float32)
    for e in range(N_EXPERTS):
        out = out + dense_gate[..., e:e + 1] * swiglu(h, w_gate[e], w_up[e], w_down[e]).astype(jnp.float32)
    return out.astype(h.dtype)


def setup_inputs(seed: int = 0) -> dict:
    key = jax.random.key(seed)
    ks = iter(jax.random.split(key, 40))
    f32 = jnp.float32

    def nrm(shape, scale):
        return jax.random.normal(next(ks), shape, f32) * scale

    NA, NS, ND, NM = N_ATTN_LAYERS, N_SSM_LAYERS, N_DENSE_LAYERS, N_MOE_LAYERS
    wb = [min(w, PAST_LEN) for w in WINDOWS]
    inp = {}
    inp['x_prompt'] = nrm((BATCH, SEQ, D_MODEL), 1.0)
    inp['x_sample'] = nrm((DEC_BATCH, DEC_SEQ, D_MODEL), 1.0)
    inp['cache_kv_w128'] = nrm((NA, DEC_BATCH, wb[0], 2, N_KV_HEADS, HEAD_DIM), 1.0)
    inp['cache_kv_w512'] = nrm((NA, DEC_BATCH, wb[1], 2, N_KV_HEADS, HEAD_DIM), 1.0)
    inp['cache_kv_w2048'] = nrm((NA, DEC_BATCH, wb[2], 2, N_KV_HEADS, HEAD_DIM), 1.0)
    inp['state_conv'] = nrm((NS, DEC_BATCH, CONV_W - 1, CONV_DIM), 1.0)
    inp['state_ssm'] = nrm((NS, DEC_BATCH, SSM_HEADS, SSM_HEAD_DIM, D_STATE), 0.1)
    inp['norm_mix'] = 1.0 + nrm((DEPTH, D_MODEL), 0.02)
    inp['norm_ffn'] = 1.0 + nrm((DEPTH, D_MODEL), 0.02)
    inp['attn_w_qkv'] = nrm((NA, D_MODEL, QKV_DIM), D_MODEL ** -0.5)
    inp['attn_q_norm'] = 1.0 + nrm((NA, HEAD_DIM), 0.02)
    inp['attn_k_norm'] = 1.0 + nrm((NA, HEAD_DIM), 0.02)
    inp['attn_w_o'] = nrm((NA, N_HEADS * HEAD_DIM, D_MODEL), (N_HEADS * HEAD_DIM) ** -0.5)
    inp['ssm_w_in'] = nrm((NS, D_MODEL, IN_DIM), D_MODEL ** -0.5)
    inp['ssm_conv_w'] = nrm((NS, CONV_W, CONV_DIM), CONV_W ** -0.5)
    inp['ssm_conv_b'] = nrm((NS, CONV_DIM), 0.02)
    u = jax.random.uniform(next(ks), (NS, SSM_HEADS), f32)
    dt0 = jnp.exp(u * (math.log(0.1) - math.log(0.001)) + math.log(0.001))
    inp['ssm_dt_bias'] = dt0 + jnp.log(-jnp.expm1(-dt0))
    inp['ssm_a_log'] = jnp.log(jax.random.uniform(next(ks), (NS, SSM_HEADS), f32, minval=1.0, maxval=16.0))
    inp['ssm_d'] = 1.0 + nrm((NS, SSM_HEADS), 0.1)
    inp['ssm_norm'] = 1.0 + nrm((NS, D_INNER), 0.02)
    inp['ssm_w_out'] = nrm((NS, D_INNER, D_MODEL), D_INNER ** -0.5)
    inp['ffn_w_gate'] = nrm((ND, D_MODEL, D_FF), D_MODEL ** -0.5)
    inp['ffn_w_up'] = nrm((ND, D_MODEL, D_FF), D_MODEL ** -0.5)
    inp['ffn_w_down'] = nrm((ND, D_FF, D_MODEL), D_FF ** -0.5)
    inp['moe_w_router'] = nrm((NM, D_MODEL, N_EXPERTS), D_MODEL ** -0.5)
    inp['moe_w_gate'] = nrm((NM, N_EXPERTS, D_MODEL, D_FF_EXPERT), D_MODEL ** -0.5)
    inp['moe_w_up'] = nrm((NM, N_EXPERTS, D_MODEL, D_FF_EXPERT), D_MODEL ** -0.5)
    inp['moe_w_down'] = nrm((NM, N_EXPERTS, D_FF_EXPERT, D_MODEL), D_FF_EXPERT ** -0.5)
    return inp


def reference(x_prompt, x_sample, cache_kv_w128, cache_kv_w512, cache_kv_w2048, state_conv, state_ssm,
              norm_mix, norm_ffn, attn_w_qkv, attn_q_norm, attn_k_norm, attn_w_o,
              ssm_w_in, ssm_conv_w, ssm_conv_b, ssm_dt_bias, ssm_a_log, ssm_d, ssm_norm, ssm_w_out,
              ffn_w_gate, ffn_w_up, ffn_w_down, moe_w_router, moe_w_gate, moe_w_up, moe_w_down):
    kv_caches = (cache_kv_w128, cache_kv_w512, cache_kv_w2048)
    xp, xs = x_prompt, x_sample
    Bp, Sp, _ = xp.shape
    Bs, Ts, _ = xs.shape
    kv_p = [[] for _ in range(N_DIL_GROUPS)]
    kv_s = [[] for _ in range(N_DIL_GROUPS)]
    conv_p, conv_s, ssm_p, ssm_s = [], [], [], []
    for i in range(DEPTH):
        j = i // 2
        hp = rmsnorm(xp, norm_mix[i])
        hs = rmsnorm(xs, norm_mix[i])
        if i % 2 == 0:
            mp, bp = attn_prompt(hp, attn_w_qkv[j], attn_q_norm[j], attn_k_norm[j], attn_w_o[j])
            ms, bs = attn_sample(hs, tuple(c[j] for c in kv_caches), attn_w_qkv[j], attn_q_norm[j],
                                 attn_k_norm[j], attn_w_o[j])
            for g in range(N_DIL_GROUPS):
                kv_p[g].append(bp[g])
                kv_s[g].append(bs[g])
        else:
            zc = jnp.zeros((Bp, CONV_W - 1, CONV_DIM), xp.dtype)
            zh = jnp.zeros((Bp, SSM_HEADS, SSM_HEAD_DIM, D_STATE), jnp.float32)
            mp, cvp, shp = ssd_mixer(hp, zc, zh, ssm_w_in[j], ssm_conv_w[j], ssm_conv_b[j], ssm_dt_bias[j],
                                     ssm_a_log[j], ssm_d[j], ssm_norm[j], ssm_w_out[j], min(SSM_CHUNK, Sp))
            ms, cvs, shs = ssd_mixer(hs, state_conv[j], state_ssm[j], ssm_w_in[j], ssm_conv_w[j], ssm_conv_b[j],
                                     ssm_dt_bias[j], ssm_a_log[j], ssm_d[j], ssm_norm[j], ssm_w_out[j], Ts)
            conv_p.append(cvp)
            ssm_p.append(shp)
            conv_s.append(cvs)
            ssm_s.append(shs)
        xp = xp + mp
        xs = xs + ms
        hp = rmsnorm(xp, norm_ffn[i])
        hs = rmsnorm(xs, norm_ffn[i])
        if i % 2 == 0:
            xp = xp + swiglu(hp, ffn_w_gate[j], ffn_w_up[j], ffn_w_down[j])
            xs = xs + swiglu(hs, ffn_w_gate[j], ffn_w_up[j], ffn_w_down[j])
        else:
            xp = xp + moe_swiglu(hp, moe_w_router[j], moe_w_gate[j], moe_w_up[j], moe_w_down[j])
            xs = xs + moe_swiglu(hs, moe_w_router[j], moe_w_gate[j], moe_w_up[j], moe_w_down[j])
    kv_w128_prompt = jnp.stack(kv_p[0], 0)
    kv_w512_prompt = jnp.stack(kv_p[1], 0)
    kv_w2048_prompt = jnp.stack(kv_p[2], 0)
    kv_w128_sample = jnp.stack(kv_s[0], 0)
    kv_w512_sample = jnp.stack(kv_s[1], 0)
    kv_w2048_sample = jnp.stack(kv_s[2], 0)
    conv_prompt = jnp.stack(conv_p, 0)
    ssm_prompt = jnp.stack(ssm_p, 0)
    conv_sample = jnp.stack(conv_s, 0)
    ssm_sample = jnp.stack(ssm_s, 0)
    return (xp, xs, kv_w128_prompt, kv_w512_prompt, kv_w2048_prompt, conv_prompt, ssm_prompt,
            kv_w128_sample, kv_w512_sample, kv_w2048_sample, conv_sample, ssm_sample)
```

```python
import functools
import math

import jax
import jax.numpy as jnp
from jax import lax
from jax.experimental import pallas as pl
from jax.experimental.pallas import tpu as pltpu

F32 = jnp.float32
BF16 = jnp.bfloat16

HEAD_DIM = 64
ROPE_DIM = 16
ROPE_THETA = 500000.0
WINDOWS = (128, 512, 2048)
DILATIONS = (1, 4, 16)
N_GROUPS = 3
KV_RATIO = 4
SSM_HEAD_DIM = 64
SSM_GROUPS = 8
D_STATE = 128
CONV_W = 4
SSM_CHUNK = 128
N_EXPERTS = 8
PAST_LEN = 16384
EPS = 1e-6

LANES = 128
SUBLANES = 8
VMEM_LIMIT_BYTES = 56 * 1024 * 1024

NEG_INF = float("-inf")


def _cparams(sem):
    return pltpu.CompilerParams(dimension_semantics=sem, vmem_limit_bytes=VMEM_LIMIT_BYTES)


def _sigmoid(x):
    return 1.0 / (1.0 + jnp.exp(-x))


def _split_bf16(x, parts):
    out = []
    r = x
    for _ in range(parts):
        h = r.astype(BF16)
        out.append(h)
        r = r - h.astype(F32)
    return out


_NN = (((1,), (0,)), ((), ()))
_NT = (((1,), (1,)), ((), ()))
_TN = (((0,), (0,)), ((), ()))


def _dot(a, b, dims=_NN, precise=False):
    mm = lambda x, y: lax.dot_general(x, y, dims, preferred_element_type=F32)
    if not precise:
        return mm(a.astype(BF16), b.astype(BF16))
    ah, al = _split_bf16(a, 2)
    bh, bl = _split_bf16(b, 2)
    return mm(ah, bh) + mm(ah, bl) + mm(al, bh)


def _rmsnorm_body(x_ref, g_ref, o_ref):
    x = x_ref[...]
    ms = jnp.mean(x * x, axis=-1, keepdims=True)
    o_ref[...] = (x * lax.rsqrt(ms + EPS) * g_ref[...]).astype(o_ref.dtype)


def rmsnorm(x, g, tm, out_dtype=None):
    m, d = x.shape
    return pl.pallas_call(
        _rmsnorm_body,
        grid=(m // tm,),
        in_specs=[pl.BlockSpec((tm, d), lambda i: (i, 0)),
                  pl.BlockSpec((1, d), lambda i: (0, 0))],
        out_specs=pl.BlockSpec((tm, d), lambda i: (i, 0)),
        out_shape=jax.ShapeDtypeStruct((m, d), BF16 if out_dtype is None else out_dtype),
        compiler_params=_cparams(("parallel",)),
        name="rmsnorm",
    )(x, g.reshape(1, d))


def _rmsnorm_router_body(x_ref, g_ref, wr_ref, h_ref, gate_ref, idx_ref, *, n_experts):
    x = x_ref[...]
    ms = jnp.mean(x * x, axis=-1, keepdims=True)
    h = x * lax.rsqrt(ms + EPS) * g_ref[...]
    h_ref[...] = h.astype(BF16)
    hh, hl = _split_bf16(h, 2)
    wh, wl = _split_bf16(wr_ref[...], 2)
    logits = (jnp.dot(hh, wh, preferred_element_type=F32)
              + jnp.dot(hh, wl, preferred_element_type=F32)
              + jnp.dot(hl, wh, preferred_element_type=F32))
    lane = lax.broadcasted_iota(jnp.int32, logits.shape, 1).astype(F32)
    l1 = jnp.where(lane < n_experts, logits, NEG_INF)
    m1 = jnp.max(l1, axis=1, keepdims=True)
    i1 = jnp.min(jnp.where(l1 == m1, lane, float(LANES)), axis=1, keepdims=True)
    l2 = jnp.where(lane == i1, NEG_INF, l1)
    m2 = jnp.max(l2, axis=1, keepdims=True)
    i2 = jnp.min(jnp.where(l2 == m2, lane, float(LANES)), axis=1, keepdims=True)
    e = jnp.exp(m2 - m1)
    g1 = 1.0 / (1.0 + e)
    g2 = e * g1
    gate_ref[...] = jnp.where(lane == 0, g1, jnp.where(lane == 1, g2, 0.0))
    idx_ref[...] = jnp.where(lane == 0, i1, jnp.where(lane == 1, i2, 0.0)).astype(jnp.int32)


def rmsnorm_router(x, g, w_router, tm):
    m, d = x.shape
    n_experts = w_router.shape[1]
    wr = jnp.pad(w_router, ((0, 0), (0, LANES - n_experts)))
    return pl.pallas_call(
        functools.partial(_rmsnorm_router_body, n_experts=n_experts),
        grid=(m // tm,),
        in_specs=[pl.BlockSpec((tm, d), lambda i: (i, 0)),
                  pl.BlockSpec((1, d), lambda i: (0, 0)),
                  pl.BlockSpec((d, LANES), lambda i: (0, 0))],
        out_specs=[pl.BlockSpec((tm, d), lambda i: (i, 0)),
                   pl.BlockSpec((tm, LANES), lambda i: (i, 0)),
                   pl.BlockSpec((tm, LANES), lambda i: (i, 0))],
        out_shape=[jax.ShapeDtypeStruct((m, d), BF16),
                   jax.ShapeDtypeStruct((m, LANES), F32),
                   jax.ShapeDtypeStruct((m, LANES), jnp.int32)],
        compiler_params=_cparams(("parallel",)),
        name="rmsnorm_router",
    )(x, g.reshape(1, d), wr)


def _mm_body(*refs, nk, has_res, precise):
    if has_res:
        a_ref, b_ref, r_ref, o_ref, acc_ref = refs
    else:
        a_ref, b_ref, o_ref, acc_ref = refs
        r_ref = None
    part = _dot(a_ref[...], b_ref[...], precise=precise)

    def finish(acc):
        if has_res:
            acc = acc + r_ref[...]
        o_ref[...] = acc.astype(o_ref.dtype)

    if nk == 1:
        finish(part)
        return
    k = pl.program_id(2)

    @pl.when(k == 0)
    def _():
        acc_ref[...] = part

    @pl.when(k > 0)
    def _():
        acc_ref[...] += part

    @pl.when(k == nk - 1)
    def _():
        finish(acc_ref[...])


def matmul(a, b, *, n=None, col_off=0, residual=None, out_dtype=F32, tm, tn, tk=None, precise=False):
    m, kdim = a.shape
    n = b.shape[1] if n is None else n
    tn = min(tn, n)
    tk = kdim if tk is None else min(tk, kdim)
    nk = kdim // tk
    assert m % tm == 0 and n % tn == 0 and kdim % tk == 0 and col_off % tn == 0
    off = col_off // tn
    in_specs = [pl.BlockSpec((tm, tk), lambda i, j, k: (i, k)),
                pl.BlockSpec((tk, tn), lambda i, j, k: (k, j + off))]
    args = [a, b]
    if residual is not None:
        in_specs.append(pl.BlockSpec((tm, tn), lambda i, j, k: (i, j)))
        args.append(residual)
    return pl.pallas_call(
        functools.partial(_mm_body, nk=nk, has_res=residual is not None, precise=precise),
        grid=(m // tm, n // tn, nk),
        in_specs=in_specs,
        out_specs=pl.BlockSpec((tm, tn), lambda i, j, k: (i, j)),
        out_shape=jax.ShapeDtypeStruct((m, n), out_dtype),
        scratch_shapes=[pltpu.VMEM((tm, tn), F32)],
        compiler_params=_cparams(("parallel", "parallel", "arbitrary")),
        name="matmul",
    )(*args)


def _proj_normrope_body(h_ref, w_ref, gain_ref, seg_ref, cos_ref, sa_ref, sb_ref, o_ref, *, n_rope_tiles, precise):
    j = pl.program_id(1)
    y = _dot(h_ref[...], w_ref[...], precise=precise)
    tn = y.shape[1]

    def norm_rope():
        ssq = sum(jnp.dot(part, seg_ref[...], preferred_element_type=F32)
                  for part in _split_bf16(y * y, 2 if precise else 1))
        yn = y * lax.rsqrt(ssq * (1.0 / HEAD_DIM) + EPS) * gain_ref[...]
        c, sa, sb = cos_ref[...], sa_ref[...], sb_ref[...]
        half = ROPE_DIM // 2
        outs = []
        for cc in range(tn // LANES):
            yc = yn[:, cc * LANES:(cc + 1) * LANES]
            outs.append(yc * c + pltpu.roll(yc, LANES - half, 1) * sa + pltpu.roll(yc, half, 1) * sb)
        o_ref[...] = jnp.concatenate(outs, axis=1).astype(o_ref.dtype)

    if n_rope_tiles is None:
        norm_rope()
    else:
        pl.when(j < n_rope_tiles)(norm_rope)

        @pl.when(j >= n_rope_tiles)
        def _():
            o_ref[...] = y.astype(o_ref.dtype)


def proj_normrope(h, w, gain, tables, *, col_off, n, n_rope_cols, out_dtype, tm, table_period_rows, precise):
    m, kdim = h.shape
    tn = 512
    assert n % tn == 0 and col_off % tn == 0 and n_rope_cols % tn == 0 and m % tm == 0
    off = col_off // tn
    n_rope_tiles = None if n_rope_cols == n else n_rope_cols // tn
    head = jnp.arange(tn) // HEAD_DIM
    seg = (head[:, None] == head[None, :]).astype(BF16)
    gain_t = jnp.tile(gain.astype(F32), tn // HEAD_DIM).reshape(1, tn)
    cos_t, sa_t, sb_t = tables
    n_tab_blocks = table_period_rows // tm
    tab_spec = pl.BlockSpec((tm, LANES), lambda i, j: (i % n_tab_blocks, 0))
    return pl.pallas_call(
        functools.partial(_proj_normrope_body, n_rope_tiles=n_rope_tiles, precise=precise),
        grid=(m // tm, n // tn),
        in_specs=[pl.BlockSpec((tm, kdim), lambda i, j: (i, 0)),
                  pl.BlockSpec((kdim, tn), lambda i, j: (0, j + off)),
                  pl.BlockSpec((1, tn), lambda i, j: (0, 0)),
                  pl.BlockSpec((tn, tn), lambda i, j: (0, 0)),
                  tab_spec, tab_spec, tab_spec],
        out_specs=pl.BlockSpec((tm, tn), lambda i, j: (i, j)),
        out_shape=jax.ShapeDtypeStruct((m, n), out_dtype),
        compiler_params=_cparams(("parallel", "parallel")),
        name="proj_normrope",
    )(h, w, gain_t, seg, cos_t, sa_t, sb_t)


def rope_tables(pos):
    half = ROPE_DIM // 2
    inv_freq = jnp.exp(-math.log(ROPE_THETA) * jnp.arange(half, dtype=F32) / half)
    ang = pos.astype(F32)[:, None] * inv_freq[None, :]
    cos, sin = jnp.cos(ang), jnp.sin(ang)
    d = jnp.arange(LANES) % HEAD_DIM
    cos_t = jnp.where(d[None, :] < ROPE_DIM, cos[:, d % half], 1.0)
    sa_t = jnp.where(d[None, :] < half, -sin[:, d % half], 0.0)
    sb_t = jnp.where((d[None, :] >= half) & (d[None, :] < ROPE_DIM), sin[:, d % half], 0.0)
    return cos_t.astype(F32), sa_t.astype(F32), sb_t.astype(F32)


def _band_attn_body(q_ref, kc_ref, kp_ref, vc_ref, vp_ref, o_ref, lse_ref, *, qb, n_kv, ratio):
    n = pl.program_id(2)
    rows = ratio * qb
    qi = lax.broadcasted_iota(jnp.int32, (rows, 2 * qb), 0) % qb
    si = lax.broadcasted_iota(jnp.int32, (rows, 2 * qb), 1)
    valid = (si >= qi) & (si <= qi + qb) & ((n > 0) | (si >= qb))
    scale = HEAD_DIM ** -0.5
    lse_ref[...] = jnp.zeros_like(lse_ref)
    for kh in range(n_kv):
        cs = slice(kh * HEAD_DIM, (kh + 1) * HEAD_DIM)
        k = jnp.concatenate([kp_ref[0, :, cs], kc_ref[0, :, cs]], axis=0).astype(BF16)
        v = jnp.concatenate([vp_ref[0, :, cs], vc_ref[0, :, cs]], axis=0).astype(BF16)
        q = jnp.concatenate(
            [q_ref[0, :, (kh * ratio + g) * HEAD_DIM:(kh * ratio + g + 1) * HEAD_DIM] for g in range(ratio)],
            axis=0)
        s = lax.dot_general(q, k, (((1,), (1,)), ((), ())), preferred_element_type=F32) * scale
        s = jnp.where(valid, s, NEG_INF)
        m = jnp.max(s, axis=1, keepdims=True)
        p = jnp.exp(s - m)
        l = jnp.sum(p, axis=1, keepdims=True)
        o = jnp.dot(p.astype(BF16), v, preferred_element_type=F32) / l
        lse = m + jnp.log(l)
        for g in range(ratio):
            hd = kh * ratio + g
            o_ref[0, :, hd * HEAD_DIM:(hd + 1) * HEAD_DIM] = o[g * qb:(g + 1) * qb]
            lse_ref[0, :, hd:hd + 1] = lse[g * qb:(g + 1) * qb]


def band_attention(q, kv, g, batch, seq):
    window, dil = WINDOWS[g], DILATIONS[g]
    qb = window // dil
    sub = seq // dil
    assert sub % qb == 0
    nb = sub // qb
    q_cols = q.shape[1] // N_GROUPS
    kv_cols = kv.shape[1] // (2 * N_GROUPS)
    n_kv = kv_cols // HEAD_DIM
    qv = q.reshape(batch, sub, dil * q.shape[1])
    kvv = kv.reshape(batch, sub, dil * kv.shape[1])
    ng2 = 2 * N_GROUPS
    q_spec = pl.BlockSpec((1, qb, q_cols), lambda b, r, n: (b, n, r * N_GROUPS + g))
    kc_spec = pl.BlockSpec((1, qb, kv_cols), lambda b, r, n: (b, n, r * ng2 + g))
    kp_spec = pl.BlockSpec((1, qb, kv_cols), lambda b, r, n: (b, jnp.maximum(n - 1, 0), r * ng2 + g))
    vc_spec = pl.BlockSpec((1, qb, kv_cols), lambda b, r, n: (b, n, r * ng2 + N_GROUPS + g))
    vp_spec = pl.BlockSpec((1, qb, kv_cols),
                           lambda b, r, n: (b, jnp.maximum(n - 1, 0), r * ng2 + N_GROUPS + g))
    o, lse = pl.pallas_call(
        functools.partial(_band_attn_body, qb=qb, n_kv=n_kv, ratio=KV_RATIO),
        grid=(batch, dil, nb),
        in_specs=[q_spec, kc_spec, kp_spec, vc_spec, vp_spec],
        out_specs=[pl.BlockSpec((1, qb, q_cols), lambda b, r, n: (b, n, r)),
                   pl.BlockSpec((1, qb, LANES), lambda b, r, n: (b, n, r))],
        out_shape=[jax.ShapeDtypeStruct((batch, sub, dil * q_cols), F32),
                   jax.ShapeDtypeStruct((batch, sub, dil * LANES), F32)],
        compiler_params=_cparams(("parallel", "parallel", "arbitrary")),
        name="band_attention",
    )(qv, kvv, kvv, kvv, kvv)
    return o.reshape(batch * seq, q_cols), lse.reshape(batch * seq, LANES)


def _gather_attn_body(q_ref, kc_ref, vc_ref, kn_ref, vn_ref, o_ref, lse_ref, *, window, dil, wb, t_new, n_kv,
                      precise):
    t = pl.program_id(1)
    r = (wb + t) % dil
    scale = HEAD_DIM ** -0.5
    q = q_ref[0, 0]
    n_heads = q.shape[0]
    s_c = _dot(q, kc_ref[0], _NT, precise) * scale
    s_n = _dot(q, kn_ref[0], _NT, precise) * scale
    mi = lax.broadcasted_iota(jnp.int32, s_c.shape, 1)
    valid_c = (mi * dil + r) >= (wb + t - window)
    ti = lax.broadcasted_iota(jnp.int32, s_n.shape, 1)
    valid_n = (ti <= t) & (lax.rem(jnp.maximum(t - ti, 0), dil) == 0) & (ti < t_new)
    s_c = jnp.where(valid_c, s_c, NEG_INF)
    s_n = jnp.where(valid_n, s_n, NEG_INF)
    m = jnp.maximum(jnp.max(s_c, axis=1, keepdims=True), jnp.max(s_n, axis=1, keepdims=True))
    p_c = jnp.exp(s_c - m)
    p_n = jnp.exp(s_n - m)
    l = jnp.sum(p_c, axis=1, keepdims=True) + jnp.sum(p_n, axis=1, keepdims=True)
    o = (_dot(p_c, vc_ref[0], _NN, precise) + _dot(p_n, vn_ref[0], _NN, precise)) / l
    col_kv = lax.broadcasted_iota(jnp.int32, o.shape, 1) // HEAD_DIM
    row_kv = lax.broadcasted_iota(jnp.int32, o.shape, 0) // (n_heads // n_kv)
    o = jnp.where(col_kv == row_kv, o, 0.0)
    acc = o[:, 0:HEAD_DIM]
    for kh in range(1, n_kv):
        acc = acc + o[:, kh * HEAD_DIM:(kh + 1) * HEAD_DIM]
    o_ref[0, 0] = acc
    lse_ref[0, 0] = jnp.broadcast_to(m + jnp.log(l), lse_ref.shape[2:])


def gather_attention(q_exp, cache, k_new, v_new, g, t_new, precise):
    window, dil = WINDOWS[g], DILATIONS[g]
    batch, wb = cache.shape[0], cache.shape[1]
    n_kv = cache.shape[3]
    kvd = n_kv * HEAD_DIM
    assert wb % dil == 0
    rows = wb // dil
    n_heads = q_exp.shape[2]
    t_pad = k_new.shape[1]
    cv = cache.reshape(batch, rows, dil * 2 * kvd)
    o, lse = pl.pallas_call(
        functools.partial(_gather_attn_body, window=window, dil=dil, wb=wb, t_new=t_new, n_kv=n_kv,
                          precise=precise),
        grid=(batch, t_new),
        in_specs=[pl.BlockSpec((1, 1, n_heads, kvd), lambda b, t: (b, t, 0, 0)),
                  pl.BlockSpec((1, rows, kvd), lambda b, t: (b, 0, ((wb + t) % dil) * 2)),
                  pl.BlockSpec((1, rows, kvd), lambda b, t: (b, 0, ((wb + t) % dil) * 2 + 1)),
                  pl.BlockSpec((1, t_pad, kvd), lambda b, t: (b, 0, 0)),
                  pl.BlockSpec((1, t_pad, kvd), lambda b, t: (b, 0, 0))],
        out_specs=[pl.BlockSpec((1, 1, n_heads, HEAD_DIM), lambda b, t: (b, t, 0, 0)),
                   pl.BlockSpec((1, 1, n_heads, LANES), lambda b, t: (b, t, 0, 0))],
        out_shape=[jax.ShapeDtypeStruct((batch, t_new, n_heads, HEAD_DIM), F32),
                   jax.ShapeDtypeStruct((batch, t_new, n_heads, LANES), F32)],
        compiler_params=_cparams(("parallel", "arbitrary")),
        name="gather_attention",
    )(q_exp, cv, cv, k_new, v_new)
    lse = jnp.pad(lse[..., 0].reshape(batch * t_new, n_heads), ((0, 0), (0, LANES - n_heads)))
    return o.reshape(batch * t_new, n_heads * HEAD_DIM), lse


def _merge_wo_body(o0_ref, o1_ref, o2_ref, l0_ref, l1_ref, l2_ref, e_ref, w_ref, x_ref, out_ref, merged_ref, *,
                   precise):
    j = pl.program_id(1)

    @pl.when(j == 0)
    def _():
        l0, l1, l2 = l0_ref[...], l1_ref[...], l2_ref[...]
        mx = jnp.maximum(jnp.maximum(l0, l1), l2)
        e0, e1, e2 = jnp.exp(l0 - mx), jnp.exp(l1 - mx), jnp.exp(l2 - mx)
        inv = 1.0 / (e0 + e1 + e2)
        acc = None
        for e, o_ref in ((e0, o0_ref), (e1, o1_ref), (e2, o2_ref)):
            wh, wl = _split_bf16(e * inv, 2)
            w_exp = (jnp.dot(wh, e_ref[...], preferred_element_type=F32)
                     + jnp.dot(wl, e_ref[...], preferred_element_type=F32))
            term = w_exp * o_ref[...]
            acc = term if acc is None else acc + term
        merged_ref[...] = acc.astype(merged_ref.dtype)

    out_ref[...] = x_ref[...] + _dot(merged_ref[...], w_ref[...], precise=precise)


def merge_wo(outs, lses, w_o, x, tm, tn=512, precise=False):
    m, d_in = outs[0].shape
    d_out = w_o.shape[1]
    n_heads = d_in // HEAD_DIM
    expand = (jnp.arange(LANES)[:, None] == (jnp.arange(d_in) // HEAD_DIM)[None, :]).astype(BF16)
    assert n_heads <= LANES
    o_spec = pl.BlockSpec((tm, d_in), lambda i, j: (i, 0))
    l_spec = pl.BlockSpec((tm, LANES), lambda i, j: (i, 0))
    return pl.pallas_call(
        functools.partial(_merge_wo_body, precise=precise),
        grid=(m // tm, d_out // tn),
        in_specs=[o_spec, o_spec, o_spec, l_spec, l_spec, l_spec,
                  pl.BlockSpec((LANES, d_in), lambda i, j: (0, 0)),
                  pl.BlockSpec((d_in, tn), lambda i, j: (0, j)),
                  pl.BlockSpec((tm, tn), lambda i, j: (i, j))],
        out_specs=pl.BlockSpec((tm, tn), lambda i, j: (i, j)),
        out_shape=jax.ShapeDtypeStruct((m, d_out), F32),
        scratch_shapes=[pltpu.VMEM((tm, d_in), F32 if precise else BF16)],
        compiler_params=_cparams(("parallel", "arbitrary")),
        name="merge_wo",
    )(*outs, *lses, expand, w_o, x)


def _gateup_body(h_ref, wg_ref, wu_ref, o_ref, *, precise=False):
    h = h_ref[...]
    g = _dot(h, wg_ref[...], precise=precise)
    u = _dot(h, wu_ref[...], precise=precise)
    o_ref[...] = (g * _sigmoid(g) * u).astype(o_ref.dtype)


def gate_up(h, w_gate, w_up, tm, tn=512, precise=False):
    m, d = h.shape
    f = w_gate.shape[1]
    return pl.pallas_call(
        functools.partial(_gateup_body, precise=precise),
        grid=(m // tm, f // tn),
        in_specs=[pl.BlockSpec((tm, d), lambda i, j: (i, 0)),
                  pl.BlockSpec((d, tn), lambda i, j: (0, j)),
                  pl.BlockSpec((d, tn), lambda i, j: (0, j))],
        out_specs=pl.BlockSpec((tm, tn), lambda i, j: (i, j)),
        out_shape=jax.ShapeDtypeStruct((m, f), F32 if precise else BF16),
        compiler_params=_cparams(("parallel", "parallel")),
        name="gate_up",
    )(h, w_gate, w_up)


def _moe_gateup_body(te_ref, na_ref, x_ref, wg_ref, wu_ref, o_ref):
    i = pl.program_id(0)

    @pl.when(i < na_ref[0])
    def _():
        _gateup_body(x_ref, wg_ref, wu_ref, o_ref)

    @pl.when(i >= na_ref[0])
    def _():
        o_ref[...] = jnp.zeros_like(o_ref)


def moe_gate_up(tile_expert, n_active, xs, w_gate, w_up, tm, tn=512):
    npad, d = xs.shape
    f = w_gate.shape[2]
    grid_spec = pltpu.PrefetchScalarGridSpec(
        num_scalar_prefetch=2,
        grid=(npad // tm, f // tn),
        in_specs=[pl.BlockSpec((tm, d), lambda i, j, te, na: (i, 0)),
                  pl.BlockSpec((None, d, tn), lambda i, j, te, na: (te[i], 0, j)),
                  pl.BlockSpec((None, d, tn), lambda i, j, te, na: (te[i], 0, j))],
        out_specs=pl.BlockSpec((tm, tn), lambda i, j, te, na: (i, j)))
    return pl.pallas_call(
        _moe_gateup_body,
        grid_spec=grid_spec,
        out_shape=jax.ShapeDtypeStruct((npad, f), BF16),
        compiler_params=_cparams(("arbitrary", "arbitrary")),
        name="moe_gate_up",
    )(tile_expert, n_active, xs, w_gate, w_up)


def _moe_down_body(te_ref, na_ref, a_ref, w_ref, gate_ref, o_ref):
    i = pl.program_id(0)

    @pl.when(i < na_ref[0])
    def _():
        o_ref[...] = jnp.dot(a_ref[...], w_ref[...], preferred_element_type=F32) * gate_ref[...]

    @pl.when(i >= na_ref[0])
    def _():
        o_ref[...] = jnp.zeros_like(o_ref)


def moe_down(tile_expert, n_active, a, w_down, slot_gate, tm, tn=512):
    npad, f = a.shape
    d = w_down.shape[2]
    grid_spec = pltpu.PrefetchScalarGridSpec(
        num_scalar_prefetch=2,
        grid=(npad // tm, d // tn),
        in_specs=[pl.BlockSpec((tm, f), lambda i, j, te, na: (i, 0)),
                  pl.BlockSpec((None, f, tn), lambda i, j, te, na: (te[i], 0, j)),
                  pl.BlockSpec((tm, 1), lambda i, j, te, na: (i, 0))],
        out_specs=pl.BlockSpec((tm, tn), lambda i, j, te, na: (i, j)))
    return pl.pallas_call(
        _moe_down_body,
        grid_spec=grid_spec,
        out_shape=jax.ShapeDtypeStruct((npad, d), F32),
        compiler_params=_cparams(("arbitrary", "arbitrary")),
        name="moe_down",
    )(tile_expert, n_active, a, w_down, slot_gate)


def moe_plan(idx, gates, tm, n_tiles):
    n_tok = idx.shape[0]
    e_flat = idx.reshape(-1)
    order = jnp.argsort(e_flat, stable=True).astype(jnp.int32)
    counts = jnp.sum((e_flat[:, None] == jnp.arange(N_EXPERTS)[None, :]).astype(jnp.int32), axis=0)
    tiles_per = (counts + tm - 1) // tm
    tile_end = jnp.cumsum(tiles_per)
    n_active = tile_end[-1]
    slot_start = (tile_end - tiles_per) * tm
    sorted_start = jnp.cumsum(counts) - counts
    tile_ids = jnp.minimum(jnp.arange(n_tiles), n_active - 1)
    tile_expert = jnp.sum((tile_ids[:, None] >= tile_end[None, :]).astype(jnp.int32), axis=1)
    slot = jnp.arange(n_tiles * tm)
    slot_e = jnp.repeat(tile_expert, tm)
    within = slot - slot_start[slot_e]
    slot_ok = (within < counts[slot_e]) & (slot < n_active * tm)
    src = order[jnp.clip(sorted_start[slot_e] + within, 0, 2 * n_tok - 1)]
    slot_token = jnp.where(slot_ok, src // 2, 0)
    slot_gate = jnp.where(slot_ok, gates.reshape(-1)[src], 0.0)
    rank = jnp.zeros((2 * n_tok,), jnp.int32).at[order].set(jnp.arange(2 * n_tok, dtype=jnp.int32))
    pos = slot_start[e_flat] + rank - sorted_start[e_flat]
    return (tile_expert.astype(jnp.int32), n_active.reshape(1).astype(jnp.int32),
            slot_token, slot_gate.reshape(-1, 1).astype(F32), pos.reshape(n_tok, 2))


def _softplus(x):
    return jnp.maximum(x, 0.0) + jnp.log(1.0 + jnp.exp(-jnp.abs(x)))


def _ssd_body(x_ref, bm_ref, cm_ref, z_ref, dt_ref, dtt_ref,
              cwx_ref, cwb_ref, cwc_ref, cbx_ref, cbb_ref, cbc_ref,
              dtb_ref, dtbt_ref, al_ref, alt_ref, dsk_ref, nw_ref,
              cix_ref, cib_ref, cic_ref, h0_ref,
              g_ref, hf_ref,
              xbuf, bbuf, cbuf, state, *, q, valid, hpg, precise):
    c = pl.program_id(2)
    nc = pl.num_programs(2)
    pd = SSM_HEAD_DIM
    halo = SUBLANES

    @pl.when(c == 0)
    def _():
        xbuf[0:halo] = cix_ref[0]
        bbuf[0:halo] = cib_ref[0]
        cbuf[0:halo] = cic_ref[0]
        state[...] = h0_ref[0]

    xbuf[halo:halo + q] = x_ref[0]
    bbuf[halo:halo + q] = bm_ref[0]
    cbuf[halo:halo + q] = cm_ref[0]

    def conv_silu(buf, cw_ref, cb_ref):
        acc = cb_ref[...]
        for j in range(CONV_W):
            lo = halo - (CONV_W - 1) + j
            acc = acc + buf[lo:lo + q] * cw_ref[j:j + 1]
        return acc * _sigmoid(acc)

    xs = conv_silu(xbuf, cwx_ref, cbx_ref)
    bmat = conv_silu(bbuf, cwb_ref, cbb_ref)
    cmat = conv_silu(cbuf, cwc_ref, cbc_ref)
    xbuf[0:halo] = xbuf[q:q + halo]
    bbuf[0:halo] = bbuf[q:q + halo]
    cbuf[0:halo] = cbuf[q:q + halo]

    dt = _softplus(dt_ref[0] + dtb_ref[...])
    dtt = _softplus(dtt_ref[0] + dtbt_ref[...])
    if valid < q:
        dt = jnp.where(lax.broadcasted_iota(jnp.int32, dt.shape, 0) < valid, dt, 0.0)
        dtt = jnp.where(lax.broadcasted_iota(jnp.int32, dtt.shape, 1) < valid, dtt, 0.0)
    a = dt * (-jnp.exp(al_ref[...]))
    at = dtt * (-jnp.exp(alt_ref[...]))
    ii = lax.broadcasted_iota(jnp.int32, (q, q), 0)
    jj = lax.broadcasted_iota(jnp.int32, (q, q), 1)
    lower = ii >= jj
    tril = lower.astype(BF16)
    triu = (ii <= jj).astype(BF16)
    a_cum = sum(jnp.dot(tril, part, preferred_element_type=F32) for part in _split_bf16(a, 3))
    a_cumt = sum(jnp.dot(part, triu, preferred_element_type=F32) for part in _split_bf16(at, 3))
    a_last = a_cum[q - 1:q, :]
    e_cum = jnp.exp(a_cum)
    e_end = jnp.exp(a_last - a_cum)
    cd_t = jnp.exp(a_cumt[:, q - 1:q])

    if not precise:
        bmat = bmat.astype(BF16)
        cmat = cmat.astype(BF16)
    cb = _dot(cmat, bmat, _NT, precise)
    y_inter = _dot(cmat, state[...], _NT, precise)

    ys = []
    for h in range(hpg):
        cs = slice(h * pd, (h + 1) * pd)
        seg = a_cum[:, h:h + 1] - a_cumt[h:h + 1, :]
        decay = jnp.exp(jnp.where(lower, seg, NEG_INF))
        xc = xs[:, cs] * dt[:, h:h + 1]
        y_h = _dot(cb * decay, xc, _NN, precise)
        ys.append(y_h + y_inter[:, cs] * e_cum[:, h:h + 1])
        upd = _dot(xc * e_end[:, h:h + 1], bmat, _TN, precise)
        state[cs, :] = state[cs, :] * cd_t[h:h + 1, :] + upd

    y = jnp.concatenate(ys, axis=1) + dsk_ref[...] * xs
    z = z_ref[0]
    gz = y * (z * _sigmoid(z))
    ms = jnp.mean(gz * gz, axis=1, keepdims=True)
    g_ref[0] = (gz * lax.rsqrt(ms + EPS) * nw_ref[...]).astype(g_ref.dtype)

    @pl.when(c == nc - 1)
    def _():
        hf_ref[0] = state[...]


def ssd_core(z, xbc, dt_raw, conv_init, h0, conv_w, conv_b, dt_bias, a_log, d_skip, norm_w, *, batch, seq, q, valid,
             precise):
    d_inner = z.shape[1]
    n_heads = dt_raw.shape[1]
    hpg = n_heads // SSM_GROUPS
    gw = hpg * SSM_HEAD_DIM
    nst = D_STATE
    assert seq % q == 0 and gw % LANES == 0
    nc = seq // q
    xb_blk = d_inner // nst
    cm_blk = xb_blk + SSM_GROUPS
    z3 = z.reshape(batch, seq, d_inner)
    xbc3 = xbc.reshape(batch, seq, xbc.shape[1])
    dt4 = dt_raw.reshape(batch, seq, SSM_GROUPS, hpg).transpose(0, 2, 1, 3)
    dtt4 = jnp.pad(dt4.transpose(0, 1, 3, 2), ((0, 0), (0, 0), (0, hpg), (0, 0)))
    pad_rows = lambda v: jnp.pad(v.reshape(SSM_GROUPS, hpg, 1), ((0, 0), (0, hpg), (0, 0)))
    dtb = dt_bias.reshape(SSM_GROUPS, 1, hpg)
    al = a_log.reshape(SSM_GROUPS, 1, hpg)
    dsk = jnp.repeat(d_skip, SSM_HEAD_DIM).reshape(1, d_inner)
    cb2 = conv_b.reshape(1, -1)
    nw = norm_w.reshape(1, d_inner)

    grp = lambda b, g, c: (b, c, g)
    in_specs = [
        pl.BlockSpec((1, q, gw), grp),
        pl.BlockSpec((1, q, nst), lambda b, g, c: (b, c, xb_blk + g)),
        pl.BlockSpec((1, q, nst), lambda b, g, c: (b, c, cm_blk + g)),
        pl.BlockSpec((1, q, gw), grp),
        pl.BlockSpec((1, None, q, hpg), lambda b, g, c: (b, g, c, 0)),
        pl.BlockSpec((1, None, 2 * hpg, q), lambda b, g, c: (b, g, 0, c)),
        pl.BlockSpec((CONV_W, gw), lambda b, g, c: (0, g)),
        pl.BlockSpec((CONV_W, nst), lambda b, g, c: (0, xb_blk + g)),
        pl.BlockSpec((CONV_W, nst), lambda b, g, c: (0, cm_blk + g)),
        pl.BlockSpec((1, gw), lambda b, g, c: (0, g)),
        pl.BlockSpec((1, nst), lambda b, g, c: (0, xb_blk + g)),
        pl.BlockSpec((1, nst), lambda b, g, c: (0, cm_blk + g)),
        pl.BlockSpec((None, 1, hpg), lambda b, g, c: (g, 0, 0)),
        pl.BlockSpec((None, 2 * hpg, 1), lambda b, g, c: (g, 0, 0)),
        pl.BlockSpec((None, 1, hpg), lambda b, g, c: (g, 0, 0)),
        pl.BlockSpec((None, 2 * hpg, 1), lambda b, g, c: (g, 0, 0)),
        pl.BlockSpec((1, gw), lambda b, g, c: (0, g)),
        pl.BlockSpec((1, gw), lambda b, g, c: (0, g)),
        pl.BlockSpec((1, SUBLANES, gw), lambda b, g, c: (b, 0, g)),
        pl.BlockSpec((1, SUBLANES, nst), lambda b, g, c: (b, 0, xb_blk + g)),
        pl.BlockSpec((1, SUBLANES, nst), lambda b, g, c: (b, 0, cm_blk + g)),
        pl.BlockSpec((1, gw, nst), lambda b, g, c: (b, g, 0)),
    ]
    g_out, h_fin = pl.pallas_call(
        functools.partial(_ssd_body, q=q, valid=valid, hpg=hpg, precise=precise),
        grid=(batch, SSM_GROUPS, nc),
        in_specs=in_specs,
        out_specs=[pl.BlockSpec((1, q, gw), grp),
                   pl.BlockSpec((1, gw, nst), lambda b, g, c: (b, g, 0))],
        out_shape=[jax.ShapeDtypeStruct((batch, seq, d_inner), F32 if precise else BF16),
                   jax.ShapeDtypeStruct(h0.shape, F32)],
        scratch_shapes=[pltpu.VMEM((q + SUBLANES, gw), F32),
                        pltpu.VMEM((q + SUBLANES, nst), F32),
                        pltpu.VMEM((q + SUBLANES, nst), F32),
                        pltpu.VMEM((gw, nst), F32)],
        compiler_params=_cparams(("parallel", "parallel", "arbitrary")),
        name="ssd_core",
    )(xbc3, xbc3, xbc3, z3, dt4, dtt4,
      conv_w, conv_w, conv_w, cb2, cb2, cb2,
      dtb, pad_rows(dt_bias), al, pad_rows(a_log), dsk, nw,
      conv_init, conv_init, conv_init, h0)
    return g_out.reshape(batch * seq, d_inner), h_fin


def _attn_tokens(h, w_qkv, q_gain, k_gain, tables, tm, table_period_rows, precise):
    q_cols = (w_qkv.shape[1] * KV_RATIO) // (KV_RATIO + 2)
    kv_cols = (w_qkv.shape[1] - q_cols) // 2
    q = proj_normrope(h, w_qkv, q_gain, tables, col_off=0, n=q_cols, n_rope_cols=q_cols,
                      out_dtype=F32 if precise else BF16, tm=tm, table_period_rows=table_period_rows,
                      precise=precise)
    kv = proj_normrope(h, w_qkv, k_gain, tables, col_off=q_cols, n=2 * kv_cols, n_rope_cols=kv_cols,
                       out_dtype=F32, tm=tm, table_period_rows=table_period_rows, precise=precise)
    return q, kv


def _kv_window(kv, g, batch, seq, keep):
    kvd = kv.shape[1] // (2 * N_GROUPS)
    kv3 = kv.reshape(batch, seq, 2 * N_GROUPS, kvd // HEAD_DIM, HEAD_DIM)
    return jnp.stack([kv3[:, seq - keep:, g], kv3[:, seq - keep:, N_GROUPS + g]], axis=2)


def kernel(x_prompt, x_sample, cache_kv_w128, cache_kv_w512, cache_kv_w2048, state_conv, state_ssm,
           norm_mix, norm_ffn, attn_w_qkv, attn_q_norm, attn_k_norm, attn_w_o,
           ssm_w_in, ssm_conv_w, ssm_conv_b, ssm_dt_bias, ssm_a_log, ssm_d, ssm_norm, ssm_w_out,
           ffn_w_gate, ffn_w_up, ffn_w_down, moe_w_router, moe_w_gate, moe_w_up, moe_w_down):
    bp, sp, d_model = x_prompt.shape
    bs, ts, _ = x_sample.shape
    depth = norm_mix.shape[0]
    mp, ms = bp * sp, bs * ts
    past_len_caches = (cache_kv_w128, cache_kv_w512, cache_kv_w2048)
    tmp = 512
    tms = ms
    tm_moe = 256
    n_heads = attn_w_o.shape[1] // HEAD_DIM
    n_kv = n_heads // KV_RATIO
    kvd = n_kv * HEAD_DIM
    d_inner = ssm_w_out.shape[1]
    ssm_heads = ssm_dt_bias.shape[1]
    conv_dim = ssm_conv_w.shape[2]
    t_pad = 16

    xp = x_prompt.reshape(mp, d_model)
    xs = x_sample.reshape(ms, d_model)

    tab_p = rope_tables(jnp.arange(sp))
    tab_s = rope_tables(PAST_LEN + (jnp.arange(ms) % ts))

    bf = lambda w: w.astype(BF16)
    kv_p = [[] for _ in range(N_GROUPS)]
    kv_s = [[] for _ in range(N_GROUPS)]
    conv_p, conv_s, ssm_p, ssm_s = [], [], [], []

    for i in range(depth):
        j = i // 2
        hp = rmsnorm(xp, norm_mix[i], tmp)
        hs = rmsnorm(xs, norm_mix[i], tms, out_dtype=F32)
        if i % 2 == 0:
            w_qkv, w_o = bf(attn_w_qkv[j]), bf(attn_w_o[j])
            q, kv = _attn_tokens(hp, w_qkv, attn_q_norm[j], attn_k_norm[j], tab_p, tmp, sp, False)
            outs, lses = [], []
            for g in range(N_GROUPS):
                o, l = band_attention(q, kv, g, bp, sp)
                outs.append(o)
                lses.append(l)
                kv_p[g].append(_kv_window(kv, g, bp, sp, min(WINDOWS[g], sp)))
            xp = merge_wo(outs, lses, w_o, xp, tm=256)
            q, kv = _attn_tokens(hs, attn_w_qkv[j], attn_q_norm[j], attn_k_norm[j], tab_s, tms, ms, True)
            q6 = q.reshape(bs, ts, N_GROUPS, n_kv, KV_RATIO, HEAD_DIM)
            kv5 = kv.reshape(bs, ts, 2 * N_GROUPS, kvd)
            eye = jnp.eye(n_kv, dtype=q.dtype)
            outs, lses = [], []
            for g in range(N_GROUPS):
                q_exp = (q6[:, :, g, :, :, None, :] * eye[None, None, :, None, :, None]).reshape(
                    bs, ts, n_heads, kvd)
                k_new = jnp.pad(kv5[:, :, g], ((0, 0), (0, t_pad - ts), (0, 0)))
                v_new = jnp.pad(kv5[:, :, N_GROUPS + g], ((0, 0), (0, t_pad - ts), (0, 0)))
                cache = past_len_caches[g][j]
                o, l = gather_attention(q_exp, cache, k_new, v_new, g, ts, True)
                outs.append(o)
                lses.append(l)
                wb = cache.shape[1]
                new_rows = jnp.stack([kv5[:, :, g], kv5[:, :, N_GROUPS + g]], axis=2).reshape(
                    bs, ts, 2, n_kv, HEAD_DIM)
                kv_s[g].append(jnp.concatenate([cache, new_rows], axis=1)[:, -wb:])
            xs = merge_wo(outs, lses, attn_w_o[j], xs, tm=tms, precise=True)
            w_gate, w_up, w_down = bf(ffn_w_gate[j]), bf(ffn_w_up[j]), bf(ffn_w_down[j])
            hp = rmsnorm(xp, norm_ffn[i], tmp)
            hs = rmsnorm(xs, norm_ffn[i], tms, out_dtype=F32)
            xp = matmul(gate_up(hp, w_gate, w_up, tmp), w_down, residual=xp, tm=tmp, tn=512, tk=2048)
            xs = matmul(gate_up(hs, ffn_w_gate[j], ffn_w_up[j], tms, precise=True), ffn_w_down[j],
                        residual=xs, tm=tms, tn=512, tk=2048, precise=True)
        else:
            def mixer(h, x, w_in, w_out, tm, batch, seq_pad, seq, conv_init, h0, precise):
                w_dt = w_in[:, d_inner + conv_dim:]
                z = matmul(h, w_in, n=d_inner, col_off=0, tm=tm, tn=512, precise=precise)
                xbc = matmul(h, w_in, n=conv_dim, col_off=d_inner, tm=tm, tn=512, precise=precise)
                dt_raw = matmul(h, w_dt, tm=tm, tn=ssm_heads, precise=precise)
                if seq_pad != seq:
                    padt = lambda v: jnp.pad(v.reshape(batch, seq, -1),
                                             ((0, 0), (0, seq_pad - seq), (0, 0))).reshape(batch * seq_pad, -1)
                    zk, xbck, dtk = padt(z), padt(xbc), padt(dt_raw)
                else:
                    zk, xbck, dtk = z, xbc, dt_raw
                y, h_fin = ssd_core(zk, xbck, dtk, conv_init, h0, ssm_conv_w[j], ssm_conv_b[j], ssm_dt_bias[j],
                                    ssm_a_log[j], ssm_d[j], ssm_norm[j], batch=batch, seq=seq_pad,
                                    q=SSM_CHUNK, valid=min(seq, SSM_CHUNK), precise=precise)
                if seq_pad != seq:
                    y = y.reshape(batch, seq_pad, d_inner)[:, :seq].reshape(batch * seq, d_inner)
                x_new = matmul(y, w_out, residual=x, tm=tm, tn=512, tk=2048, precise=precise)
                return x_new, xbc.reshape(batch, seq, conv_dim), h_fin

            zero_conv = jnp.zeros((bp, SUBLANES, conv_dim), F32)
            zero_h = jnp.zeros((bp, ssm_heads * SSM_HEAD_DIM, D_STATE), F32)
            xp, xbc_p, hfin_p = mixer(hp, xp, bf(ssm_w_in[j]), bf(ssm_w_out[j]), tmp, bp, sp, sp,
                                      zero_conv, zero_h, False)
            conv_p.append(xbc_p[:, sp - (CONV_W - 1):])
            ssm_p.append(hfin_p.reshape(bp, ssm_heads, SSM_HEAD_DIM, D_STATE))

            prev = state_conv[j]
            conv_init_s = jnp.pad(prev, ((0, 0), (SUBLANES - (CONV_W - 1), 0), (0, 0)))
            h0_s = state_ssm[j].reshape(bs, ssm_heads * SSM_HEAD_DIM, D_STATE)
            xs, xbc_s, hfin_s = mixer(hs, xs, ssm_w_in[j], ssm_w_out[j], tms, bs, SSM_CHUNK, ts,
                                      conv_init_s, h0_s, True)
            conv_s.append(jnp.concatenate([prev, xbc_s], axis=1)[:, -(CONV_W - 1):])
            ssm_s.append(hfin_s.reshape(bs, ssm_heads, SSM_HEAD_DIM, D_STATE))

            hp, gate_p, idx_p = rmsnorm_router(xp, norm_ffn[i], moe_w_router[j], tmp)
            hs, gate_s, idx_s = rmsnorm_router(xs, norm_ffn[i], moe_w_router[j], tms)
            h_all = jnp.concatenate([hp, hs], axis=0)
            gates = jnp.concatenate([gate_p[:, :2], gate_s[:, :2]], axis=0)
            idx = jnp.concatenate([idx_p[:, :2], idx_s[:, :2]], axis=0)
            n_tok = mp + ms
            n_tiles = (2 * n_tok + N_EXPERTS * (tm_moe - 1)) // tm_moe
            tile_expert, n_active, slot_token, slot_gate, pos = moe_plan(idx, gates, tm_moe, n_tiles)
            x_sorted = jnp.take(h_all, slot_token, axis=0)
            act = moe_gate_up(tile_expert, n_active, x_sorted, bf(moe_w_gate[j]), bf(moe_w_up[j]), tm_moe)
            y_sorted = moe_down(tile_expert, n_active, act, bf(moe_w_down[j]), slot_gate, tm_moe)
            moe_out = jnp.take(y_sorted, pos[:, 0], axis=0) + jnp.take(y_sorted, pos[:, 1], axis=0)
            xp = xp + moe_out[:mp]
            xs = xs + moe_out[mp:]

    stack = lambda xs_: jnp.stack(xs_, 0)
    return (xp.reshape(bp, sp, d_model), xs.reshape(bs, ts, d_model),
            stack(kv_p[0]), stack(kv_p[1]), stack(kv_p[2]), stack(conv_p), stack(ssm_p),
            stack(kv_s[0]), stack(kv_s[1]), stack(kv_s[2]), stack(conv_s), stack(ssm_s))
```

```python
import functools
import math

import jax
import jax.numpy as jnp
from jax import lax
from jax.experimental import pallas as pl
from jax.experimental.pallas import tpu as pltpu

F32 = jnp.float32
BF16 = jnp.bfloat16

HEAD_DIM = 64
ROPE_DIM = 16
ROPE_THETA = 500000.0
WINDOWS = (128, 512, 2048)
DILATIONS = (1, 4, 16)
N_GROUPS = 3
KV_RATIO = 4
SSM_HEAD_DIM = 64
SSM_GROUPS = 8
D_STATE = 128
CONV_W = 4
SSM_CHUNK = 128
N_EXPERTS = 8
PAST_LEN = 16384
EPS = 1e-6

LANES = 128
SUBLANES = 8
VMEM_LIMIT_BYTES = 56 * 1024 * 1024

NEG_INF = float("-inf")


def _cparams(sem):
    return pltpu.CompilerParams(dimension_semantics=sem, vmem_limit_bytes=VMEM_LIMIT_BYTES)


def _sigmoid(x):
    return 1.0 / (1.0 + jnp.exp(-x))


def _split_bf16(x, parts):
    out = []
    r = x
    for _ in range(parts):
        h = r.astype(BF16)
        out.append(h)
        r = r - h.astype(F32)
    return out


_NN = (((1,), (0,)), ((), ()))
_NT = (((1,), (1,)), ((), ()))
_TN = (((0,), (0,)), ((), ()))


def _dot(a, b, dims=_NN, precise=False):
    mm = lambda x, y: lax.dot_general(x, y, dims, preferred_element_type=F32)
    if not precise:
        return mm(a.astype(BF16), b.astype(BF16))
    ah, al = _split_bf16(a, 2)
    bh, bl = _split_bf16(b, 2)
    return mm(ah, bh) + mm(ah, bl) + mm(al, bh)


def _rmsnorm_body(x_ref, g_ref, o_ref):
    x = x_ref[...]
    ms = jnp.mean(x * x, axis=-1, keepdims=True)
    o_ref[...] = (x * lax.rsqrt(ms + EPS) * g_ref[...]).astype(o_ref.dtype)


def rmsnorm(x, g, tm, out_dtype=None):
    m, d = x.shape
    return pl.pallas_call(
        _rmsnorm_body,
        grid=(m // tm,),
        in_specs=[pl.BlockSpec((tm, d), lambda i: (i, 0)),
                  pl.BlockSpec((1, d), lambda i: (0, 0))],
        out_specs=pl.BlockSpec((tm, d), lambda i: (i, 0)),
        out_shape=jax.ShapeDtypeStruct((m, d), BF16 if out_dtype is None else out_dtype),
        compiler_params=_cparams(("parallel",)),
        name="rmsnorm",
    )(x, g.reshape(1, d))


def _rmsnorm_router_body(x_ref, g_ref, wr_ref, h_ref, gate_ref, idx_ref, *, n_experts):
    x = x_ref[...]
    ms = jnp.mean(x * x, axis=-1, keepdims=True)
    h = x * lax.rsqrt(ms + EPS) * g_ref[...]
    h_ref[...] = h.astype(BF16)
    hh, hl = _split_bf16(h, 2)
    wh, wl = _split_bf16(wr_ref[...], 2)
    logits = (jnp.dot(hh, wh, preferred_element_type=F32)
              + jnp.dot(hh, wl, preferred_element_type=F32)
              + jnp.dot(hl, wh, preferred_element_type=F32))
    lane = lax.broadcasted_iota(jnp.int32, logits.shape, 1).astype(F32)
    l1 = jnp.where(lane < n_experts, logits, NEG_INF)
    m1 = jnp.max(l1, axis=1, keepdims=True)
    i1 = jnp.min(jnp.where(l1 == m1, lane, float(LANES)), axis=1, keepdims=True)
    l2 = jnp.where(lane == i1, NEG_INF, l1)
    m2 = jnp.max(l2, axis=1, keepdims=True)
    i2 = jnp.min(jnp.where(l2 == m2, lane, float(LANES)), axis=1, keepdims=True)
    e = jnp.exp(m2 - m1)
    g1 = 1.0 / (1.0 + e)
    g2 = e * g1
    gate_ref[...] = jnp.where(lane == 0, g1, jnp.where(lane == 1, g2, 0.0))
    idx_ref[...] = jnp.where(lane == 0, i1, jnp.where(lane == 1, i2, 0.0)).astype(jnp.int32)


def rmsnorm_router(x, g, w_router, tm):
    m, d = x.shape
    n_experts = w_router.shape[1]
    wr = jnp.pad(w_router, ((0, 0), (0, LANES - n_experts)))
    return pl.pallas_call(
        functools.partial(_rmsnorm_router_body, n_experts=n_experts),
        grid=(m // tm,),
        in_specs=[pl.BlockSpec((tm, d), lambda i: (i, 0)),
                  pl.BlockSpec((1, d), lambda i: (0, 0)),
                  pl.BlockSpec((d, LANES), lambda i: (0, 0))],
        out_specs=[pl.BlockSpec((tm, d), lambda i: (i, 0)),
                   pl.BlockSpec((tm, LANES), lambda i: (i, 0)),
                   pl.BlockSpec((tm, LANES), lambda i: (i, 0))],
        out_shape=[jax.ShapeDtypeStruct((m, d), BF16),
                   jax.ShapeDtypeStruct((m, LANES), F32),
                   jax.ShapeDtypeStruct((m, LANES), jnp.int32)],
        compiler_params=_cparams(("parallel",)),
        name="rmsnorm_router",
    )(x, g.reshape(1, d), wr)


def _weight_tile(w_ref, wbf_ref, precise):
    if precise:
        return w_ref[...]

    @pl.when(pl.program_id(1) == 0)
    def _():
        wbf_ref[...] = w_ref[...].astype(BF16)

    return wbf_ref[...]


def _weight_scratch(kdim, tn, precise):
    return pltpu.VMEM((2 * SUBLANES, LANES) if precise else (kdim, tn), BF16)


def _mm_body(*refs, has_res, precise):
    if has_res:
        a_ref, w_ref, r_ref, o_ref, wbf_ref = refs
    else:
        a_ref, w_ref, o_ref, wbf_ref = refs
        r_ref = None
    acc = _dot(a_ref[...], _weight_tile(w_ref, wbf_ref, precise), precise=precise)
    if has_res:
        acc = acc + r_ref[...]
    o_ref[...] = acc.astype(o_ref.dtype)


def matmul(a, w, layer, *, n=None, col_off=0, residual=None, out_dtype=F32, tm, tn, precise=False):
    m, kdim = a.shape
    n = w.shape[2] if n is None else n
    tn = min(tn, n)
    assert m % tm == 0 and n % tn == 0 and col_off % tn == 0 and w.shape[1] == kdim
    off = col_off // tn
    in_specs = [pl.BlockSpec((tm, kdim), lambda j, i: (i, 0)),
                pl.BlockSpec((None, kdim, tn), lambda j, i: (layer, 0, j + off))]
    args = [a, w]
    if residual is not None:
        in_specs.append(pl.BlockSpec((tm, tn), lambda j, i: (i, j)))
        args.append(residual)
    return pl.pallas_call(
        functools.partial(_mm_body, has_res=residual is not None, precise=precise),
        grid=(n // tn, m // tm),
        in_specs=in_specs,
        out_specs=pl.BlockSpec((tm, tn), lambda j, i: (i, j)),
        out_shape=jax.ShapeDtypeStruct((m, n), out_dtype),
        scratch_shapes=[_weight_scratch(kdim, tn, precise)],
        compiler_params=_cparams(("parallel", "arbitrary")),
        name="matmul",
    )(*args)


def _proj_normrope_body(h_ref, w_ref, gain_ref, seg_ref, cos_ref, sa_ref, sb_ref, o_ref, wbf_ref, *,
                        n_rope_tiles, precise):
    j = pl.program_id(0)
    y = _dot(h_ref[...], _weight_tile(w_ref, wbf_ref, precise), precise=precise)
    tn = y.shape[1]

    def norm_rope():
        ssq = sum(jnp.dot(part, seg_ref[...], preferred_element_type=F32)
                  for part in _split_bf16(y * y, 2 if precise else 1))
        yn = y * lax.rsqrt(ssq * (1.0 / HEAD_DIM) + EPS) * gain_ref[...]
        c, sa, sb = cos_ref[...], sa_ref[...], sb_ref[...]
        half = ROPE_DIM // 2
        for cc in range(tn // LANES):
            yc = yn[:, cc * LANES:(cc + 1) * LANES]
            o_ref[cc] = yc * c + pltpu.roll(yc, LANES - half, 1) * sa + pltpu.roll(yc, half, 1) * sb

    if n_rope_tiles is None:
        norm_rope()
    else:
        pl.when(j < n_rope_tiles)(norm_rope)

        @pl.when(j >= n_rope_tiles)
        def _():
            for cc in range(tn // LANES):
                o_ref[cc] = y[:, cc * LANES:(cc + 1) * LANES]


def proj_normrope(h, w, layer, gain, tables, *, col_off, n, n_rope_cols, tm, table_period_rows, precise):
    m, kdim = h.shape
    tn = 512
    assert n % tn == 0 and col_off % tn == 0 and n_rope_cols % tn == 0 and m % tm == 0
    off = col_off // tn
    n_rope_tiles = None if n_rope_cols == n else n_rope_cols // tn
    head = jnp.arange(tn) // HEAD_DIM
    seg = (head[:, None] == head[None, :]).astype(BF16)
    gain_t = jnp.tile(gain.astype(F32), tn // HEAD_DIM).reshape(1, tn)
    cos_t, sa_t, sb_t = tables
    n_tab_blocks = table_period_rows // tm
    tab_spec = pl.BlockSpec((tm, LANES), lambda j, i: (i % n_tab_blocks, 0))
    return pl.pallas_call(
        functools.partial(_proj_normrope_body, n_rope_tiles=n_rope_tiles, precise=precise),
        grid=(n // tn, m // tm),
        in_specs=[pl.BlockSpec((tm, kdim), lambda j, i: (i, 0)),
                  pl.BlockSpec((None, kdim, tn), lambda j, i: (layer, 0, j + off)),
                  pl.BlockSpec((1, tn), lambda j, i: (0, 0)),
                  pl.BlockSpec((tn, tn), lambda j, i: (0, 0)),
                  tab_spec, tab_spec, tab_spec],
        out_specs=pl.BlockSpec((tn // LANES, tm, LANES), lambda j, i: (j, i, 0)),
        out_shape=jax.ShapeDtypeStruct((n // LANES, m, LANES), F32),
        scratch_shapes=[_weight_scratch(kdim, tn, precise)],
        compiler_params=_cparams(("parallel", "arbitrary")),
        name="proj_normrope",
    )(h, w, gain_t, seg, cos_t, sa_t, sb_t)


def rope_tables(pos):
    half = ROPE_DIM // 2
    inv_freq = jnp.exp(-math.log(ROPE_THETA) * jnp.arange(half, dtype=F32) / half)
    ang = pos.astype(F32)[:, None] * inv_freq[None, :]
    cos, sin = jnp.cos(ang), jnp.sin(ang)
    d = jnp.arange(LANES) % HEAD_DIM
    cos_t = jnp.where(d[None, :] < ROPE_DIM, cos[:, d % half], 1.0)
    sa_t = jnp.where(d[None, :] < half, -sin[:, d % half], 0.0)
    sb_t = jnp.where((d[None, :] >= half) & (d[None, :] < ROPE_DIM), sin[:, d % half], 0.0)
    return cos_t.astype(F32), sa_t.astype(F32), sb_t.astype(F32)


def _band_attn_body(q_ref, kc_ref, kp_ref, vc_ref, vp_ref, o_ref, lse_ref, *, qb, dil, nblk, ratio):
    n = pl.program_id(1)
    pair = pl.program_id(2)
    win = qb * dil
    kv_per_step = LANES // HEAD_DIM
    cols_per_kv = ratio * HEAD_DIM // LANES
    rows = 2 * cols_per_kv * qb
    qi = lax.broadcasted_iota(jnp.int32, (rows, 2 * qb), 0) % qb
    si = lax.broadcasted_iota(jnp.int32, (rows, 2 * qb), 1)
    band = (si >= qi) & (si <= qi + qb)
    band_first = band & ((n > 0) | (si >= qb))
    lane = lax.broadcasted_iota(jnp.int32, (qb, LANES), 1)
    lo = lane < HEAD_DIM
    lo2 = lax.broadcasted_iota(jnp.int32, (2 * qb, LANES), 1) < HEAD_DIM
    scale = HEAD_DIM ** -0.5

    @pl.when(pair == 0)
    def _():
        lse_ref[...] = jnp.zeros_like(lse_ref)

    for s in range(nblk):
        base = s * win
        valid = band_first if s == 0 else band

        def residue(r, carry, base=base, s=s, valid=valid):
            take = lambda ref, b: ref[pl.ds(b + r, qb, stride=dil), :]
            if s == 0:
                k_prev, v_prev = take(kp_ref, 0), take(vp_ref, 0)
            else:
                k_prev, v_prev = take(kc_ref, base - win), take(vc_ref, base - win)
            k2 = jnp.concatenate([k_prev, take(kc_ref, base)], axis=0)
            v2 = jnp.concatenate([v_prev, take(vc_ref, base)], axis=0)
            k2r = pltpu.roll(k2, HEAD_DIM, 1)
            v2r = pltpu.roll(v2, HEAD_DIM, 1)
            lse_tile = lse_ref[pl.ds(base + r, qb, stride=dil), :]
            for c in range(kv_per_step):
                kd = (jnp.where(lo2, k2, k2r) if c == 0 else jnp.where(lo2, k2r, k2)).astype(BF16)
                vd = (jnp.where(lo2, v2, v2r) if c == 0 else jnp.where(lo2, v2r, v2)).astype(BF16)
                parts = []
                for u in range(cols_per_kv):
                    col = c * cols_per_kv + u
                    qc = q_ref[col, pl.ds(base + r, qb, stride=dil), :]
                    parts.append(jnp.where(lo, qc, 0.0))
                    parts.append(jnp.where(lo, 0.0, qc))
                q4 = jnp.concatenate(parts, axis=0).astype(BF16)
                sc = lax.dot_general(q4, kd, _NT, preferred_element_type=F32) * scale
                sc = jnp.where(valid, sc, NEG_INF)
                m = jnp.max(sc, axis=1, keepdims=True)
                p = jnp.exp(sc - m)
                l = jnp.sum(p, axis=1, keepdims=True)
                o4 = jnp.dot(p.astype(BF16), vd, preferred_element_type=F32) / l
                lse4 = m + jnp.log(l)
                for u in range(cols_per_kv):
                    col = c * cols_per_kv + u
                    ra = (2 * u) * qb
                    rb = (2 * u + 1) * qb
                    o_ref[col, pl.ds(base + r, qb, stride=dil), :] = jnp.where(lo, o4[ra:ra + qb], o4[rb:rb + qb])
                    head = pair * (2 * ratio) + 2 * col
                    lse_tile = jnp.where(lane == head, lse4[ra:ra + qb], lse_tile)
                    lse_tile = jnp.where(lane == head + 1, lse4[rb:rb + qb], lse_tile)
            lse_ref[pl.ds(base + r, qb, stride=dil), :] = lse_tile
            return carry

        if dil == 1:
            residue(0, 0)
        else:
            lax.fori_loop(0, dil, residue, 0)


def band_attention(q, kv, g, batch, seq):
    window, dil = WINDOWS[g], DILATIONS[g]
    qb = window // dil
    assert seq % window == 0
    nblk = max(1, 512 // window)
    step_rows = nblk * window
    assert seq % step_rows == 0
    steps = seq // step_rows
    wins = seq // window
    q_planes = q.shape[0] // N_GROUPS
    kv_planes = kv.shape[0] // (2 * N_GROUPS)
    q_blk = q_planes // kv_planes
    kk = g * kv_planes
    kvv = (N_GROUPS + g) * kv_planes
    prev_row = lambda b, n: b * wins + jnp.maximum(n * nblk - 1, 0)
    o, lse = pl.pallas_call(
        functools.partial(_band_attn_body, qb=qb, dil=dil, nblk=nblk, ratio=KV_RATIO),
        grid=(batch, steps, kv_planes),
        in_specs=[pl.BlockSpec((q_blk, step_rows, LANES), lambda b, n, p: (g * kv_planes + p, b * steps + n, 0)),
                  pl.BlockSpec((None, step_rows, LANES), lambda b, n, p: (kk + p, b * steps + n, 0)),
                  pl.BlockSpec((None, window, LANES), lambda b, n, p: (kk + p, prev_row(b, n), 0)),
                  pl.BlockSpec((None, step_rows, LANES), lambda b, n, p: (kvv + p, b * steps + n, 0)),
                  pl.BlockSpec((None, window, LANES), lambda b, n, p: (kvv + p, prev_row(b, n), 0))],
        out_specs=[pl.BlockSpec((q_blk, step_rows, LANES), lambda b, n, p: (p, b * steps + n, 0)),
                   pl.BlockSpec((step_rows, LANES), lambda b, n, p: (b * steps + n, 0))],
        out_shape=[jax.ShapeDtypeStruct((q_planes, batch * seq, LANES), F32),
                   jax.ShapeDtypeStruct((batch * seq, LANES), F32)],
        compiler_params=_cparams(("parallel", "parallel", "arbitrary")),
        name="band_attention",
    )(q, kv, kv, kv, kv)
    return o, lse


def _gather_attn_body(q_ref, kc_ref, vc_ref, kn_ref, vn_ref, o_ref, lse_ref, *, window, dil, wb, t_new, n_kv,
                      precise):
    t = pl.program_id(1)
    r = (wb + t) % dil
    scale = HEAD_DIM ** -0.5
    q = q_ref[0, 0]
    n_heads = q.shape[0]
    s_c = _dot(q, kc_ref[0], _NT, precise) * scale
    s_n = _dot(q, kn_ref[0], _NT, precise) * scale
    mi = lax.broadcasted_iota(jnp.int32, s_c.shape, 1)
    valid_c = (mi * dil + r) >= (wb + t - window)
    ti = lax.broadcasted_iota(jnp.int32, s_n.shape, 1)
    valid_n = (ti <= t) & (lax.rem(jnp.maximum(t - ti, 0), dil) == 0) & (ti < t_new)
    s_c = jnp.where(valid_c, s_c, NEG_INF)
    s_n = jnp.where(valid_n, s_n, NEG_INF)
    m = jnp.maximum(jnp.max(s_c, axis=1, keepdims=True), jnp.max(s_n, axis=1, keepdims=True))
    p_c = jnp.exp(s_c - m)
    p_n = jnp.exp(s_n - m)
    l = jnp.sum(p_c, axis=1, keepdims=True) + jnp.sum(p_n, axis=1, keepdims=True)
    o = (_dot(p_c, vc_ref[0], _NN, precise) + _dot(p_n, vn_ref[0], _NN, precise)) / l
    col_kv = lax.broadcasted_iota(jnp.int32, o.shape, 1) // HEAD_DIM
    row_kv = lax.broadcasted_iota(jnp.int32, o.shape, 0) // (n_heads // n_kv)
    o = jnp.where(col_kv == row_kv, o, 0.0)
    acc = o[:, 0:HEAD_DIM]
    for kh in range(1, n_kv):
        acc = acc + o[:, kh * HEAD_DIM:(kh + 1) * HEAD_DIM]
    o_ref[0, 0] = acc
    lse_ref[0, 0] = jnp.broadcast_to(m + jnp.log(l), lse_ref.shape[2:])


def gather_attention(q_exp, cache, k_new, v_new, g, t_new, precise):
    window, dil = WINDOWS[g], DILATIONS[g]
    batch, wb = cache.shape[0], cache.shape[1]
    n_kv = cache.shape[3]
    kvd = n_kv * HEAD_DIM
    assert wb % dil == 0
    rows = wb // dil
    n_heads = q_exp.shape[2]
    t_pad = k_new.shape[1]
    cv = cache.reshape(batch, rows, dil * 2 * kvd)
    o, lse = pl.pallas_call(
        functools.partial(_gather_attn_body, window=window, dil=dil, wb=wb, t_new=t_new, n_kv=n_kv,
                          precise=precise),
        grid=(batch, t_new),
        in_specs=[pl.BlockSpec((1, 1, n_heads, kvd), lambda b, t: (b, t, 0, 0)),
                  pl.BlockSpec((1, rows, kvd), lambda b, t: (b, 0, ((wb + t) % dil) * 2)),
                  pl.BlockSpec((1, rows, kvd), lambda b, t: (b, 0, ((wb + t) % dil) * 2 + 1)),
                  pl.BlockSpec((1, t_pad, kvd), lambda b, t: (b, 0, 0)),
                  pl.BlockSpec((1, t_pad, kvd), lambda b, t: (b, 0, 0))],
        out_specs=[pl.BlockSpec((1, 1, n_heads, HEAD_DIM), lambda b, t: (b, t, 0, 0)),
                   pl.BlockSpec((1, 1, n_heads, LANES), lambda b, t: (b, t, 0, 0))],
        out_shape=[jax.ShapeDtypeStruct((batch, t_new, n_heads, HEAD_DIM), F32),
                   jax.ShapeDtypeStruct((batch, t_new, n_heads, LANES), F32)],
        compiler_params=_cparams(("parallel", "arbitrary")),
        name="gather_attention",
    )(q_exp, cv, cv, k_new, v_new)
    lse = jnp.pad(lse[..., 0].reshape(batch * t_new, n_heads), ((0, 0), (0, LANES - n_heads)))
    return o.reshape(batch * t_new, n_heads * HEAD_DIM), lse


def _merge_body(o0_ref, o1_ref, o2_ref, l0_ref, l1_ref, l2_ref, e_ref, out_ref):
    l0, l1, l2 = l0_ref[...], l1_ref[...], l2_ref[...]
    mx = jnp.maximum(jnp.maximum(l0, l1), l2)
    e0, e1, e2 = jnp.exp(l0 - mx), jnp.exp(l1 - mx), jnp.exp(l2 - mx)
    inv = 1.0 / (e0 + e1 + e2)
    w_exps = []
    for e in (e0, e1, e2):
        wh, wl = _split_bf16(e * inv, 2)
        w_exps.append(jnp.dot(wh, e_ref[...], preferred_element_type=F32)
                      + jnp.dot(wl, e_ref[...], preferred_element_type=F32))
    for c in range(o0_ref.shape[0]):
        cs = slice(c * LANES, (c + 1) * LANES)
        acc = (w_exps[0][:, cs] * o0_ref[c] + w_exps[1][:, cs] * o1_ref[c] + w_exps[2][:, cs] * o2_ref[c])
        out_ref[:, cs] = acc.astype(out_ref.dtype)


def merge_groups(outs, lses, tm, out_dtype):
    planes, m, _ = outs[0].shape
    d_in = planes * LANES
    assert d_in // HEAD_DIM <= LANES
    expand = (jnp.arange(LANES)[:, None] == (jnp.arange(d_in) // HEAD_DIM)[None, :]).astype(BF16)
    o_spec = pl.BlockSpec((planes, tm, LANES), lambda i: (0, i, 0))
    l_spec = pl.BlockSpec((tm, LANES), lambda i: (i, 0))
    return pl.pallas_call(
        _merge_body,
        grid=(m // tm,),
        in_specs=[o_spec, o_spec, o_spec, l_spec, l_spec, l_spec,
                  pl.BlockSpec((LANES, d_in), lambda i: (0, 0))],
        out_specs=pl.BlockSpec((tm, d_in), lambda i: (i, 0)),
        out_shape=jax.ShapeDtypeStruct((m, d_in), out_dtype),
        compiler_params=_cparams(("parallel",)),
        name="merge_groups",
    )(*outs, *lses, expand)


def _gateup_body(h_ref, wg_ref, wu_ref, o_ref, wgbf_ref, wubf_ref, *, precise):
    h = h_ref[...]
    g = _dot(h, _weight_tile(wg_ref, wgbf_ref, precise), precise=precise)
    u = _dot(h, _weight_tile(wu_ref, wubf_ref, precise), precise=precise)
    o_ref[...] = (g * _sigmoid(g) * u).astype(o_ref.dtype)


def gate_up(h, w_gate, w_up, layer, tm, tn=512, precise=False):
    m, d = h.shape
    f = w_gate.shape[2]
    w_spec = pl.BlockSpec((None, d, tn), lambda j, i: (layer, 0, j))
    return pl.pallas_call(
        functools.partial(_gateup_body, precise=precise),
        grid=(f // tn, m // tm),
        in_specs=[pl.BlockSpec((tm, d), lambda j, i: (i, 0)), w_spec, w_spec],
        out_specs=pl.BlockSpec((tm, tn), lambda j, i: (i, j)),
        out_shape=jax.ShapeDtypeStruct((m, f), F32 if precise else BF16),
        scratch_shapes=[_weight_scratch(d, tn, precise), _weight_scratch(d, tn, precise)],
        compiler_params=_cparams(("parallel", "arbitrary")),
        name="gate_up",
    )(h, w_gate, w_up)


def _expert_changed(te_ref):
    i = pl.program_id(1)
    return (i == 0) | (te_ref[i] != te_ref[jnp.maximum(i - 1, 0)])


def _moe_gateup_body(te_ref, na_ref, x_ref, wg_ref, wu_ref, o_ref, wgbf_ref, wubf_ref):
    i = pl.program_id(1)

    @pl.when(_expert_changed(te_ref))
    def _():
        wgbf_ref[...] = wg_ref[...].astype(BF16)
        wubf_ref[...] = wu_ref[...].astype(BF16)

    @pl.when(i < na_ref[0])
    def _():
        x = x_ref[...]
        g = jnp.dot(x, wgbf_ref[...], preferred_element_type=F32)
        u = jnp.dot(x, wubf_ref[...], preferred_element_type=F32)
        o_ref[...] = (g * _sigmoid(g) * u).astype(o_ref.dtype)

    @pl.when(i >= na_ref[0])
    def _():
        o_ref[...] = jnp.zeros_like(o_ref)


def moe_gate_up(tile_expert, n_active, xs, w_gate, w_up, layer, tm, tn=512):
    npad, d = xs.shape
    f = w_gate.shape[3]
    w_spec = pl.BlockSpec((None, None, d, tn), lambda j, i, te, na: (layer, te[i], 0, j))
    grid_spec = pltpu.PrefetchScalarGridSpec(
        num_scalar_prefetch=2,
        grid=(f // tn, npad // tm),
        in_specs=[pl.BlockSpec((tm, d), lambda j, i, te, na: (i, 0)), w_spec, w_spec],
        out_specs=pl.BlockSpec((tm, tn), lambda j, i, te, na: (i, j)),
        scratch_shapes=[pltpu.VMEM((d, tn), BF16), pltpu.VMEM((d, tn), BF16)])
    return pl.pallas_call(
        _moe_gateup_body,
        grid_spec=grid_spec,
        out_shape=jax.ShapeDtypeStruct((npad, f), BF16),
        compiler_params=_cparams(("arbitrary", "arbitrary")),
        name="moe_gate_up",
    )(tile_expert, n_active, xs, w_gate, w_up)


def _moe_down_body(te_ref, na_ref, a_ref, w_ref, gate_ref, o_ref, wbf_ref):
    i = pl.program_id(1)

    @pl.when(_expert_changed(te_ref))
    def _():
        wbf_ref[...] = w_ref[...].astype(BF16)

    @pl.when(i < na_ref[0])
    def _():
        o_ref[...] = jnp.dot(a_ref[...], wbf_ref[...], preferred_element_type=F32) * gate_ref[...]

    @pl.when(i >= na_ref[0])
    def _():
        o_ref[...] = jnp.zeros_like(o_ref)


def moe_down(tile_expert, n_active, a, w_down, slot_gate, layer, tm, tn=512):
    npad, f = a.shape
    d = w_down.shape[3]
    grid_spec = pltpu.PrefetchScalarGridSpec(
        num_scalar_prefetch=2,
        grid=(d // tn, npad // tm),
        in_specs=[pl.BlockSpec((tm, f), lambda j, i, te, na: (i, 0)),
                  pl.BlockSpec((None, None, f, tn), lambda j, i, te, na: (layer, te[i], 0, j)),
                  pl.BlockSpec((tm, 1), lambda j, i, te, na: (i, 0))],
        out_specs=pl.BlockSpec((tm, tn), lambda j, i, te, na: (i, j)),
        scratch_shapes=[pltpu.VMEM((f, tn), BF16)])
    return pl.pallas_call(
        _moe_down_body,
        grid_spec=grid_spec,
        out_shape=jax.ShapeDtypeStruct((npad, d), F32),
        compiler_params=_cparams(("arbitrary", "arbitrary")),
        name="moe_down",
    )(tile_expert, n_active, a, w_down, slot_gate)


def moe_plan(idx, gates, tm, n_tiles):
    n_tok = idx.shape[0]
    e_flat = idx.reshape(-1)
    order = jnp.argsort(e_flat, stable=True).astype(jnp.int32)
    onehot = (e_flat[:, None] == jnp.arange(N_EXPERTS)[None, :]).astype(jnp.int32)
    counts = jnp.sum(onehot, axis=0)
    within_expert = jnp.sum((jnp.cumsum(onehot, axis=0) - onehot) * onehot, axis=1)
    tiles_per = (counts + tm - 1) // tm
    tile_end = jnp.cumsum(tiles_per)
    n_active = tile_end[-1]
    slot_start = (tile_end - tiles_per) * tm
    sorted_start = jnp.cumsum(counts) - counts
    tile_ids = jnp.minimum(jnp.arange(n_tiles), n_active - 1)
    tile_expert = jnp.sum((tile_ids[:, None] >= tile_end[None, :]).astype(jnp.int32), axis=1)
    slot = jnp.arange(n_tiles * tm)
    slot_e = jnp.repeat(tile_expert, tm)
    within = slot - slot_start[slot_e]
    slot_ok = (within < counts[slot_e]) & (slot < n_active * tm)
    src = order[jnp.clip(sorted_start[slot_e] + within, 0, 2 * n_tok - 1)]
    slot_token = jnp.where(slot_ok, src // 2, 0)
    slot_gate = jnp.where(slot_ok, gates.reshape(-1)[src], 0.0)
    pos = slot_start[e_flat] + within_expert
    return (tile_expert.astype(jnp.int32), n_active.reshape(1).astype(jnp.int32),
            slot_token, slot_gate.reshape(-1, 1).astype(F32), pos.reshape(n_tok, 2))


def _softplus(x):
    return jnp.maximum(x, 0.0) + jnp.log(1.0 + jnp.exp(-jnp.abs(x)))


def _ssd_body(x_ref, bm_ref, cm_ref, z_ref, dt_ref, dtt_ref,
              cwx_ref, cwb_ref, cwc_ref, cbx_ref, cbb_ref, cbc_ref,
              dtb_ref, dtbt_ref, al_ref, alt_ref, dsk_ref, nw_ref,
              cix_ref, cib_ref, cic_ref, h0_ref,
              g_ref, hf_ref,
              xbuf, bbuf, cbuf, state, *, q, valid, hpg, precise):
    c = pl.program_id(2)
    nc = pl.num_programs(2)
    pd = SSM_HEAD_DIM
    halo = SUBLANES

    @pl.when(c == 0)
    def _():
        xbuf[0:halo] = cix_ref[0]
        bbuf[0:halo] = cib_ref[0]
        cbuf[0:halo] = cic_ref[0]
        state[...] = h0_ref[0]

    xbuf[halo:halo + q] = x_ref[0]
    bbuf[halo:halo + q] = bm_ref[0]
    cbuf[halo:halo + q] = cm_ref[0]

    def conv_silu(buf, cw_ref, cb_ref):
        acc = cb_ref[...]
        for j in range(CONV_W):
            lo = halo - (CONV_W - 1) + j
            acc = acc + buf[lo:lo + q] * cw_ref[j:j + 1]
        return acc * _sigmoid(acc)

    xs = conv_silu(xbuf, cwx_ref, cbx_ref)
    bmat = conv_silu(bbuf, cwb_ref, cbb_ref)
    cmat = conv_silu(cbuf, cwc_ref, cbc_ref)
    xbuf[0:halo] = xbuf[q:q + halo]
    bbuf[0:halo] = bbuf[q:q + halo]
    cbuf[0:halo] = cbuf[q:q + halo]

    dt = _softplus(dt_ref[0] + dtb_ref[...])
    dtt = _softplus(dtt_ref[0] + dtbt_ref[...])
    if valid < q:
        dt = jnp.where(lax.broadcasted_iota(jnp.int32, dt.shape, 0) < valid, dt, 0.0)
        dtt = jnp.where(lax.broadcasted_iota(jnp.int32, dtt.shape, 1) < valid, dtt, 0.0)
    a = dt * (-jnp.exp(al_ref[...]))
    at = dtt * (-jnp.exp(alt_ref[...]))
    ii = lax.broadcasted_iota(jnp.int32, (q, q), 0)
    jj = lax.broadcasted_iota(jnp.int32, (q, q), 1)
    lower = ii >= jj
    tril = lower.astype(BF16)
    triu = (ii <= jj).astype(BF16)
    a_cum = sum(jnp.dot(tril, part, preferred_element_type=F32) for part in _split_bf16(a, 3))
    a_cumt = sum(jnp.dot(part, triu, preferred_element_type=F32) for part in _split_bf16(at, 3))
    a_last = a_cum[q - 1:q, :]
    e_cum = jnp.exp(a_cum)
    e_end = jnp.exp(a_last - a_cum)
    cd_t = jnp.exp(a_cumt[:, q - 1:q])

    if not precise:
        bmat = bmat.astype(BF16)
        cmat = cmat.astype(BF16)
    cb = _dot(cmat, bmat, _NT, precise)
    y_inter = _dot(cmat, state[...], _NT, precise)

    ys = []
    for h in range(hpg):
        cs = slice(h * pd, (h + 1) * pd)
        seg = a_cum[:, h:h + 1] - a_cumt[h:h + 1, :]
        decay = jnp.exp(jnp.where(lower, seg, NEG_INF))
        xc = xs[:, cs] * dt[:, h:h + 1]
        y_h = _dot(cb * decay, xc, _NN, precise)
        ys.append(y_h + y_inter[:, cs] * e_cum[:, h:h + 1])
        upd = _dot(xc * e_end[:, h:h + 1], bmat, _TN, precise)
        state[cs, :] = state[cs, :] * cd_t[h:h + 1, :] + upd

    y = jnp.concatenate(ys, axis=1) + dsk_ref[...] * xs
    z = z_ref[0]
    gz = y * (z * _sigmoid(z))
    ms = jnp.mean(gz * gz, axis=1, keepdims=True)
    g_ref[0] = (gz * lax.rsqrt(ms + EPS) * nw_ref[...]).astype(g_ref.dtype)

    @pl.when(c == nc - 1)
    def _():
        hf_ref[0] = state[...]


def ssd_core(z, xbc, dt_raw, conv_init, h0, conv_w, conv_b, dt_bias, a_log, d_skip, norm_w, *, batch, seq, q, valid,
             precise):
    d_inner = z.shape[1]
    n_heads = dt_raw.shape[1]
    hpg = n_heads // SSM_GROUPS
    gw = hpg * SSM_HEAD_DIM
    nst = D_STATE
    assert seq % q == 0 and gw % LANES == 0
    nc = seq // q
    xb_blk = d_inner // nst
    cm_blk = xb_blk + SSM_GROUPS
    z3 = z.reshape(batch, seq, d_inner)
    xbc3 = xbc.reshape(batch, seq, xbc.shape[1])
    dt4 = dt_raw.reshape(batch, seq, SSM_GROUPS, hpg).transpose(0, 2, 1, 3)
    dtt4 = jnp.pad(dt4.transpose(0, 1, 3, 2), ((0, 0), (0, 0), (0, hpg), (0, 0)))
    pad_rows = lambda v: jnp.pad(v.reshape(SSM_GROUPS, hpg, 1), ((0, 0), (0, hpg), (0, 0)))
    dtb = dt_bias.reshape(SSM_GROUPS, 1, hpg)
    al = a_log.reshape(SSM_GROUPS, 1, hpg)
    dsk = jnp.repeat(d_skip, SSM_HEAD_DIM).reshape(1, d_inner)
    cb2 = conv_b.reshape(1, -1)
    nw = norm_w.reshape(1, d_inner)

    grp = lambda b, g, c: (b, c, g)
    in_specs = [
        pl.BlockSpec((1, q, gw), grp),
        pl.BlockSpec((1, q, nst), lambda b, g, c: (b, c, xb_blk + g)),
        pl.BlockSpec((1, q, nst), lambda b, g, c: (b, c, cm_blk + g)),
        pl.BlockSpec((1, q, gw), grp),
        pl.BlockSpec((1, None, q, hpg), lambda b, g, c: (b, g, c, 0)),
        pl.BlockSpec((1, None, 2 * hpg, q), lambda b, g, c: (b, g, 0, c)),
        pl.BlockSpec((CONV_W, gw), lambda b, g, c: (0, g)),
        pl.BlockSpec((CONV_W, nst), lambda b, g, c: (0, xb_blk + g)),
        pl.BlockSpec((CONV_W, nst), lambda b, g, c: (0, cm_blk + g)),
        pl.BlockSpec((1, gw), lambda b, g, c: (0, g)),
        pl.BlockSpec((1, nst), lambda b, g, c: (0, xb_blk + g)),
        pl.BlockSpec((1, nst), lambda b, g, c: (0, cm_blk + g)),
        pl.BlockSpec((None, 1, hpg), lambda b, g, c: (g, 0, 0)),
        pl.BlockSpec((None, 2 * hpg, 1), lambda b, g, c: (g, 0, 0)),
        pl.BlockSpec((None, 1, hpg), lambda b, g, c: (g, 0, 0)),
        pl.BlockSpec((None, 2 * hpg, 1), lambda b, g, c: (g, 0, 0)),
        pl.BlockSpec((1, gw), lambda b, g, c: (0, g)),
        pl.BlockSpec((1, gw), lambda b, g, c: (0, g)),
        pl.BlockSpec((1, SUBLANES, gw), lambda b, g, c: (b, 0, g)),
        pl.BlockSpec((1, SUBLANES, nst), lambda b, g, c: (b, 0, xb_blk + g)),
        pl.BlockSpec((1, SUBLANES, nst), lambda b, g, c: (b, 0, cm_blk + g)),
        pl.BlockSpec((1, gw, nst), lambda b, g, c: (b, g, 0)),
    ]
    g_out, h_fin = pl.pallas_call(
        functools.partial(_ssd_body, q=q, valid=valid, hpg=hpg, precise=precise),
        grid=(batch, SSM_GROUPS, nc),
        in_specs=in_specs,
        out_specs=[pl.BlockSpec((1, q, gw), grp),
                   pl.BlockSpec((1, gw, nst), lambda b, g, c: (b, g, 0))],
        out_shape=[jax.ShapeDtypeStruct((batch, seq, d_inner), F32 if precise else BF16),
                   jax.ShapeDtypeStruct(h0.shape, F32)],
        scratch_shapes=[pltpu.VMEM((q + SUBLANES, gw), F32),
                        pltpu.VMEM((q + SUBLANES, nst), F32),
                        pltpu.VMEM((q + SUBLANES, nst), F32),
                        pltpu.VMEM((gw, nst), F32)],
        compiler_params=_cparams(("parallel", "parallel", "arbitrary")),
        name="ssd_core",
    )(xbc3, xbc3, xbc3, z3, dt4, dtt4,
      conv_w, conv_w, conv_w, cb2, cb2, cb2,
      dtb, pad_rows(dt_bias), al, pad_rows(a_log), dsk, nw,
      conv_init, conv_init, conv_init, h0)
    return g_out.reshape(batch * seq, d_inner), h_fin


def _attn_tokens(h, w_qkv, layer, q_gain, k_gain, tables, tm, table_period_rows, precise):
    q_cols = (w_qkv.shape[2] * KV_RATIO) // (KV_RATIO + 2)
    kv_cols = (w_qkv.shape[2] - q_cols) // 2
    q = proj_normrope(h, w_qkv, layer, q_gain, tables, col_off=0, n=q_cols, n_rope_cols=q_cols,
                      tm=tm, table_period_rows=table_period_rows, precise=precise)
    kv = proj_normrope(h, w_qkv, layer, k_gain, tables, col_off=q_cols, n=2 * kv_cols, n_rope_cols=kv_cols,
                       tm=tm, table_period_rows=table_period_rows, precise=precise)
    return q, kv


def _kv_window(kv, g, batch, seq, keep):
    per = kv.shape[0] // (2 * N_GROUPS)
    kv4 = kv.reshape(2 * N_GROUPS, per, batch, seq, LANES)[:, :, :, seq - keep:]
    kv_g = jnp.stack([kv4[g], kv4[N_GROUPS + g]], axis=0)
    return kv_g.transpose(2, 3, 0, 1, 4).reshape(batch, keep, 2, per * LANES // HEAD_DIM, HEAD_DIM)


def _planes_to_rows(x):
    return x.transpose(1, 0, 2).reshape(x.shape[1], x.shape[0] * LANES)


def _rows_to_planes(x):
    return x.reshape(x.shape[0], x.shape[1] // LANES, LANES).transpose(1, 0, 2)


def kernel(x_prompt, x_sample, cache_kv_w128, cache_kv_w512, cache_kv_w2048, state_conv, state_ssm,
           norm_mix, norm_ffn, attn_w_qkv, attn_q_norm, attn_k_norm, attn_w_o,
           ssm_w_in, ssm_conv_w, ssm_conv_b, ssm_dt_bias, ssm_a_log, ssm_d, ssm_norm, ssm_w_out,
           ffn_w_gate, ffn_w_up, ffn_w_down, moe_w_router, moe_w_gate, moe_w_up, moe_w_down):
    bp, sp, d_model = x_prompt.shape
    bs, ts, _ = x_sample.shape
    depth = norm_mix.shape[0]
    mp, ms = bp * sp, bs * ts
    past_len_caches = (cache_kv_w128, cache_kv_w512, cache_kv_w2048)
    tmp = 512
    tms = ms
    tm_moe = 256
    n_heads = attn_w_o.shape[1] // HEAD_DIM
    n_kv = n_heads // KV_RATIO
    kvd = n_kv * HEAD_DIM
    d_inner = ssm_w_out.shape[1]
    ssm_heads = ssm_dt_bias.shape[1]
    conv_dim = ssm_conv_w.shape[2]
    t_pad = 16

    xp = x_prompt.reshape(mp, d_model)
    xs = x_sample.reshape(ms, d_model)

    tab_p = rope_tables(jnp.arange(sp))
    tab_s = rope_tables(PAST_LEN + (jnp.arange(ms) % ts))

    kv_p = [[] for _ in range(N_GROUPS)]
    kv_s = [[] for _ in range(N_GROUPS)]
    conv_p, conv_s, ssm_p, ssm_s = [], [], [], []

    for i in range(depth):
        j = i // 2
        hp = rmsnorm(xp, norm_mix[i], tmp)
        hs = rmsnorm(xs, norm_mix[i], tms, out_dtype=F32)
        if i % 2 == 0:
            q, kv = _attn_tokens(hp, attn_w_qkv, j, attn_q_norm[j], attn_k_norm[j], tab_p, tmp, sp, False)
            outs, lses = [], []
            for g in range(N_GROUPS):
                o, l = band_attention(q, kv, g, bp, sp)
                outs.append(o)
                lses.append(l)
                kv_p[g].append(_kv_window(kv, g, bp, sp, min(WINDOWS[g], sp)))
            xp = matmul(merge_groups(outs, lses, 256, BF16), attn_w_o, j, residual=xp, tm=tmp, tn=512)
            q, kv = _attn_tokens(hs, attn_w_qkv, j, attn_q_norm[j], attn_k_norm[j], tab_s, tms, ms, True)
            q6 = _planes_to_rows(q).reshape(bs, ts, N_GROUPS, n_kv, KV_RATIO, HEAD_DIM)
            kv5 = _planes_to_rows(kv).reshape(bs, ts, 2 * N_GROUPS, kvd)
            eye = jnp.eye(n_kv, dtype=q.dtype)
            outs, lses = [], []
            for g in range(N_GROUPS):
                q_exp = (q6[:, :, g, :, :, None, :] * eye[None, None, :, None, :, None]).reshape(
                    bs, ts, n_heads, kvd)
                k_new = jnp.pad(kv5[:, :, g], ((0, 0), (0, t_pad - ts), (0, 0)))
                v_new = jnp.pad(kv5[:, :, N_GROUPS + g], ((0, 0), (0, t_pad - ts), (0, 0)))
                cache = past_len_caches[g][j]
                o, l = gather_attention(q_exp, cache, k_new, v_new, g, ts, True)
                outs.append(_rows_to_planes(o))
                lses.append(l)
                wb = cache.shape[1]
                new_rows = jnp.stack([kv5[:, :, g], kv5[:, :, N_GROUPS + g]], axis=2).reshape(
                    bs, ts, 2, n_kv, HEAD_DIM)
                kv_s[g].append(jnp.concatenate([cache, new_rows], axis=1)[:, -wb:])
            xs = matmul(merge_groups(outs, lses, tms, F32), attn_w_o, j, residual=xs, tm=tms, tn=512,
                        precise=True)
            hp = rmsnorm(xp, norm_ffn[i], tmp)
            hs = rmsnorm(xs, norm_ffn[i], tms, out_dtype=F32)
            xp = matmul(gate_up(hp, ffn_w_gate, ffn_w_up, j, tmp), ffn_w_down, j, residual=xp, tm=tmp, tn=256)
            xs = matmul(gate_up(hs, ffn_w_gate, ffn_w_up, j, tms, precise=True), ffn_w_down, j,
                        residual=xs, tm=tms, tn=256, precise=True)
        else:
            w_dt = ssm_w_in[j:j + 1, :, d_inner + conv_dim:]

            def mixer(h, x, tm, batch, seq_pad, seq, conv_init, h0, precise):
                z = matmul(h, ssm_w_in, j, n=d_inner, col_off=0, tm=tm, tn=512, precise=precise)
                xbc = matmul(h, ssm_w_in, j, n=conv_dim, col_off=d_inner, tm=tm, tn=512, precise=precise)
                dt_raw = matmul(h, w_dt, 0, tm=tm, tn=ssm_heads, precise=precise)
                if seq_pad != seq:
                    padt = lambda v: jnp.pad(v.reshape(batch, seq, -1),
                                             ((0, 0), (0, seq_pad - seq), (0, 0))).reshape(batch * seq_pad, -1)
                    zk, xbck, dtk = padt(z), padt(xbc), padt(dt_raw)
                else:
                    zk, xbck, dtk = z, xbc, dt_raw
                y, h_fin = ssd_core(zk, xbck, dtk, conv_init, h0, ssm_conv_w[j], ssm_conv_b[j], ssm_dt_bias[j],
                                    ssm_a_log[j], ssm_d[j], ssm_norm[j], batch=batch, seq=seq_pad,
                                    q=SSM_CHUNK, valid=min(seq, SSM_CHUNK), precise=precise)
                if seq_pad != seq:
                    y = y.reshape(batch, seq_pad, d_inner)[:, :seq].reshape(batch * seq, d_inner)
                x_new = matmul(y, ssm_w_out, j, residual=x, tm=tm, tn=512, precise=precise)
                return x_new, xbc.reshape(batch, seq, conv_dim), h_fin

            zero_conv = jnp.zeros((bp, SUBLANES, conv_dim), F32)
            zero_h = jnp.zeros((bp, ssm_heads * SSM_HEAD_DIM, D_STATE), F32)
            xp, xbc_p, hfin_p = mixer(hp, xp, tmp, bp, sp, sp, zero_conv, zero_h, False)
            conv_p.append(xbc_p[:, sp - (CONV_W - 1):])
            ssm_p.append(hfin_p.reshape(bp, ssm_heads, SSM_HEAD_DIM, D_STATE))

            prev = state_conv[j]
            conv_init_s = jnp.pad(prev, ((0, 0), (SUBLANES - (CONV_W - 1), 0), (0, 0)))
            h0_s = state_ssm[j].reshape(bs, ssm_heads * SSM_HEAD_DIM, D_STATE)
            xs, xbc_s, hfin_s = mixer(hs, xs, tms, bs, SSM_CHUNK, ts, conv_init_s, h0_s, True)
            conv_s.append(jnp.concatenate([prev, xbc_s], axis=1)[:, -(CONV_W - 1):])
            ssm_s.append(hfin_s.reshape(bs, ssm_heads, SSM_HEAD_DIM, D_STATE))

            hp, gate_p, idx_p = rmsnorm_router(xp, norm_ffn[i], moe_w_router[j], tmp)
            hs, gate_s, idx_s = rmsnorm_router(xs, norm_ffn[i], moe_w_router[j], tms)
            h_all = jnp.concatenate([hp, hs], axis=0)
            gates = jnp.concatenate([gate_p[:, :2], gate_s[:, :2]], axis=0)
            idx = jnp.concatenate([idx_p[:, :2], idx_s[:, :2]], axis=0)
            n_tok = mp + ms
            n_tiles = (2 * n_tok + N_EXPERTS * (tm_moe - 1)) // tm_moe
            tile_expert, n_active, slot_token, slot_gate, pos = moe_plan(idx, gates, tm_moe, n_tiles)
            x_sorted = jnp.take(h_all, slot_token, axis=0)
            act = moe_gate_up(tile_expert, n_active, x_sorted, moe_w_gate, moe_w_up, j, tm_moe)
            y_sorted = moe_down(tile_expert, n_active, act, moe_w_down, slot_gate, j, tm_moe)
            moe_out = jnp.take(y_sorted, pos[:, 0], axis=0) + jnp.take(y_sorted, pos[:, 1], axis=0)
            xp = xp + moe_out[:mp]
            xs = xs + moe_out[mp:]

    stack = lambda xs_: jnp.stack(xs_, 0)
    return (xp.reshape(bp, sp, d_model), xs.reshape(bs, ts, d_model),
            stack(kv_p[0]), stack(kv_p[1]), stack(kv_p[2]), stack(conv_p), stack(ssm_p),
            stack(kv_s[0]), stack(kv_s[1]), stack(kv_s[2]), stack(conv_s), stack(ssm_s))
```

```python
import functools
import math

import jax
import jax.numpy as jnp
from jax import lax
from jax.experimental import pallas as pl
from jax.experimental.pallas import tpu as pltpu

F32 = jnp.float32
BF16 = jnp.bfloat16

HEAD_DIM = 64
ROPE_DIM = 16
ROPE_THETA = 500000.0
WINDOWS = (128, 512, 2048)
DILATIONS = (1, 4, 16)
N_GROUPS = 3
KV_RATIO = 4
SSM_HEAD_DIM = 64
SSM_GROUPS = 8
D_STATE = 128
CONV_W = 4
SSM_CHUNK = 128
N_EXPERTS = 8
PAST_LEN = 16384
EPS = 1e-6

LANES = 128
SUBLANES = 8
VMEM_LIMIT_BYTES = 56 * 1024 * 1024

NEG_INF = float("-inf")


def _cparams(sem):
    return pltpu.CompilerParams(dimension_semantics=sem, vmem_limit_bytes=VMEM_LIMIT_BYTES)


def _sigmoid(x):
    return 1.0 / (1.0 + jnp.exp(-x))


def _split_bf16(x, parts):
    out = []
    r = x
    for _ in range(parts):
        h = r.astype(BF16)
        out.append(h)
        r = r - h.astype(F32)
    return out


_NN = (((1,), (0,)), ((), ()))
_NT = (((1,), (1,)), ((), ()))
_TN = (((0,), (0,)), ((), ()))


def _dot(a, b, dims=_NN, precise=False):
    mm = lambda x, y: lax.dot_general(x, y, dims, preferred_element_type=F32)
    if not precise:
        return mm(a.astype(BF16), b.astype(BF16))
    ah, al = _split_bf16(a, 2)
    bh, bl = _split_bf16(b, 2)
    return mm(ah, bh) + mm(ah, bl) + mm(al, bh)


def _rmsnorm_body(x_ref, g_ref, o_ref):
    x = x_ref[...]
    ms = jnp.mean(x * x, axis=-1, keepdims=True)
    o_ref[...] = (x * lax.rsqrt(ms + EPS) * g_ref[...]).astype(o_ref.dtype)


def rmsnorm(x, g, tm, out_dtype=None):
    m, d = x.shape
    return pl.pallas_call(
        _rmsnorm_body,
        grid=(m // tm,),
        in_specs=[pl.BlockSpec((tm, d), lambda i: (i, 0)),
                  pl.BlockSpec((1, d), lambda i: (0, 0))],
        out_specs=pl.BlockSpec((tm, d), lambda i: (i, 0)),
        out_shape=jax.ShapeDtypeStruct((m, d), BF16 if out_dtype is None else out_dtype),
        compiler_params=_cparams(("parallel",)),
        name="rmsnorm",
    )(x, g.reshape(1, d))


def _rmsnorm_router_body(x_ref, g_ref, wr_ref, h_ref, gate_ref, idx_ref, *, n_experts):
    x = x_ref[...]
    ms = jnp.mean(x * x, axis=-1, keepdims=True)
    h = x * lax.rsqrt(ms + EPS) * g_ref[...]
    h_ref[...] = h
    hh, hl = _split_bf16(h, 2)
    wh, wl = _split_bf16(wr_ref[...], 2)
    logits = (jnp.dot(hh, wh, preferred_element_type=F32)
              + jnp.dot(hh, wl, preferred_element_type=F32)
              + jnp.dot(hl, wh, preferred_element_type=F32))
    lane = lax.broadcasted_iota(jnp.int32, logits.shape, 1).astype(F32)
    l1 = jnp.where(lane < n_experts, logits, NEG_INF)
    m1 = jnp.max(l1, axis=1, keepdims=True)
    i1 = jnp.min(jnp.where(l1 == m1, lane, float(LANES)), axis=1, keepdims=True)
    l2 = jnp.where(lane == i1, NEG_INF, l1)
    m2 = jnp.max(l2, axis=1, keepdims=True)
    i2 = jnp.min(jnp.where(l2 == m2, lane, float(LANES)), axis=1, keepdims=True)
    e = jnp.exp(m2 - m1)
    g1 = 1.0 / (1.0 + e)
    g2 = e * g1
    gate_ref[...] = jnp.where(lane == 0, g1, jnp.where(lane == 1, g2, 0.0))
    idx_ref[...] = jnp.where(lane == 0, i1, jnp.where(lane == 1, i2, 0.0)).astype(jnp.int32)


def rmsnorm_router(x, g, w_router, tm):
    m, d = x.shape
    n_experts = w_router.shape[1]
    wr = jnp.pad(w_router, ((0, 0), (0, LANES - n_experts)))
    return pl.pallas_call(
        functools.partial(_rmsnorm_router_body, n_experts=n_experts),
        grid=(m // tm,),
        in_specs=[pl.BlockSpec((tm, d), lambda i: (i, 0)),
                  pl.BlockSpec((1, d), lambda i: (0, 0)),
                  pl.BlockSpec((d, LANES), lambda i: (0, 0))],
        out_specs=[pl.BlockSpec((tm, d), lambda i: (i, 0)),
                   pl.BlockSpec((tm, LANES), lambda i: (i, 0)),
                   pl.BlockSpec((tm, LANES), lambda i: (i, 0))],
        out_shape=[jax.ShapeDtypeStruct((m, d), F32),
                   jax.ShapeDtypeStruct((m, LANES), F32),
                   jax.ShapeDtypeStruct((m, LANES), jnp.int32)],
        compiler_params=_cparams(("parallel",)),
        name="rmsnorm_router",
    )(x, g.reshape(1, d), wr)


def _weight_tile(w_ref, wbf_ref, precise):
    if precise:
        return w_ref[...]

    @pl.when(pl.program_id(1) == 0)
    def _():
        wbf_ref[...] = w_ref[...].astype(BF16)

    return wbf_ref[...]


def _weight_scratch(kdim, tn, precise):
    return pltpu.VMEM((2 * SUBLANES, LANES) if precise else (kdim, tn), BF16)


def _mm_body(*refs, has_res, precise):
    if has_res:
        a_ref, w_ref, r_ref, o_ref, wbf_ref = refs
    else:
        a_ref, w_ref, o_ref, wbf_ref = refs
        r_ref = None
    acc = _dot(a_ref[...], _weight_tile(w_ref, wbf_ref, precise), precise=precise)
    if has_res:
        acc = acc + r_ref[...]
    o_ref[...] = acc.astype(o_ref.dtype)


def matmul(a, w, layer, *, n=None, col_off=0, residual=None, out_dtype=F32, tm, tn, precise=False):
    m, kdim = a.shape
    n = w.shape[2] if n is None else n
    tn = min(tn, n)
    assert m % tm == 0 and n % tn == 0 and col_off % tn == 0 and w.shape[1] == kdim
    off = col_off // tn
    in_specs = [pl.BlockSpec((tm, kdim), lambda j, i: (i, 0)),
                pl.BlockSpec((None, kdim, tn), lambda j, i: (layer, 0, j + off))]
    args = [a, w]
    if residual is not None:
        in_specs.append(pl.BlockSpec((tm, tn), lambda j, i: (i, j)))
        args.append(residual)
    return pl.pallas_call(
        functools.partial(_mm_body, has_res=residual is not None, precise=precise),
        grid=(n // tn, m // tm),
        in_specs=in_specs,
        out_specs=pl.BlockSpec((tm, tn), lambda j, i: (i, j)),
        out_shape=jax.ShapeDtypeStruct((m, n), out_dtype),
        scratch_shapes=[_weight_scratch(kdim, tn, precise)],
        compiler_params=_cparams(("parallel", "arbitrary")),
        name="matmul",
    )(*args)


def _proj_normrope_body(h_ref, w_ref, gain_ref, seg_ref, cos_ref, sa_ref, sb_ref, o_ref, wbf_ref, *,
                        n_rope_tiles, precise):
    j = pl.program_id(0)
    y = _dot(h_ref[...], _weight_tile(w_ref, wbf_ref, precise), precise=precise)
    tn = y.shape[1]

    def norm_rope():
        ssq = sum(jnp.dot(part, seg_ref[...], preferred_element_type=F32)
                  for part in _split_bf16(y * y, 2 if precise else 1))
        yn = y * lax.rsqrt(ssq * (1.0 / HEAD_DIM) + EPS) * gain_ref[...]
        c, sa, sb = cos_ref[...], sa_ref[...], sb_ref[...]
        half = ROPE_DIM // 2
        for cc in range(tn // LANES):
            yc = yn[:, cc * LANES:(cc + 1) * LANES]
            o_ref[cc] = yc * c + pltpu.roll(yc, LANES - half, 1) * sa + pltpu.roll(yc, half, 1) * sb

    if n_rope_tiles is None:
        norm_rope()
    else:
        pl.when(j < n_rope_tiles)(norm_rope)

        @pl.when(j >= n_rope_tiles)
        def _():
            for cc in range(tn // LANES):
                o_ref[cc] = y[:, cc * LANES:(cc + 1) * LANES]


def proj_normrope(h, w, layer, gain, tables, *, col_off, n, n_rope_cols, tm, table_period_rows, precise):
    m, kdim = h.shape
    tn = 512
    assert n % tn == 0 and col_off % tn == 0 and n_rope_cols % tn == 0 and m % tm == 0
    off = col_off // tn
    n_rope_tiles = None if n_rope_cols == n else n_rope_cols // tn
    head = jnp.arange(tn) // HEAD_DIM
    seg = (head[:, None] == head[None, :]).astype(BF16)
    gain_t = jnp.tile(gain.astype(F32), tn // HEAD_DIM).reshape(1, tn)
    cos_t, sa_t, sb_t = tables
    n_tab_blocks = table_period_rows // tm
    tab_spec = pl.BlockSpec((tm, LANES), lambda j, i: (i % n_tab_blocks, 0))
    return pl.pallas_call(
        functools.partial(_proj_normrope_body, n_rope_tiles=n_rope_tiles, precise=precise),
        grid=(n // tn, m // tm),
        in_specs=[pl.BlockSpec((tm, kdim), lambda j, i: (i, 0)),
                  pl.BlockSpec((None, kdim, tn), lambda j, i: (layer, 0, j + off)),
                  pl.BlockSpec((1, tn), lambda j, i: (0, 0)),
                  pl.BlockSpec((tn, tn), lambda j, i: (0, 0)),
                  tab_spec, tab_spec, tab_spec],
        out_specs=pl.BlockSpec((tn // LANES, tm, LANES), lambda j, i: (j, i, 0)),
        out_shape=jax.ShapeDtypeStruct((n // LANES, m, LANES), F32),
        scratch_shapes=[_weight_scratch(kdim, tn, precise)],
        compiler_params=_cparams(("parallel", "arbitrary")),
        name="proj_normrope",
    )(h, w, gain_t, seg, cos_t, sa_t, sb_t)


def rope_tables(pos):
    half = ROPE_DIM // 2
    inv_freq = jnp.exp(-math.log(ROPE_THETA) * jnp.arange(half, dtype=F32) / half)
    ang = pos.astype(F32)[:, None] * inv_freq[None, :]
    cos, sin = jnp.cos(ang), jnp.sin(ang)
    d = jnp.arange(LANES) % HEAD_DIM
    cos_t = jnp.where(d[None, :] < ROPE_DIM, cos[:, d % half], 1.0)
    sa_t = jnp.where(d[None, :] < half, -sin[:, d % half], 0.0)
    sb_t = jnp.where((d[None, :] >= half) & (d[None, :] < ROPE_DIM), sin[:, d % half], 0.0)
    return cos_t.astype(F32), sa_t.astype(F32), sb_t.astype(F32)


def _band_attn_body(q_ref, kc_ref, kp_ref, vc_ref, vp_ref, o_ref, lse_ref, *, qb, dil, nblk, ratio):
    n = pl.program_id(1)
    pair = pl.program_id(2)
    win = qb * dil
    kv_per_step = LANES // HEAD_DIM
    cols_per_kv = ratio * HEAD_DIM // LANES
    rows = 2 * cols_per_kv * qb
    qi = lax.broadcasted_iota(jnp.int32, (rows, 2 * qb), 0) % qb
    si = lax.broadcasted_iota(jnp.int32, (rows, 2 * qb), 1)
    band = (si >= qi) & (si <= qi + qb)
    band_first = band & ((n > 0) | (si >= qb))
    lane = lax.broadcasted_iota(jnp.int32, (qb, LANES), 1)
    lo = lane < HEAD_DIM
    lo2 = lax.broadcasted_iota(jnp.int32, (2 * qb, LANES), 1) < HEAD_DIM
    scale = HEAD_DIM ** -0.5

    @pl.when(pair == 0)
    def _():
        lse_ref[...] = jnp.zeros_like(lse_ref)

    for s in range(nblk):
        base = s * win
        valid = band_first if s == 0 else band

        def residue(r, carry, base=base, s=s, valid=valid):
            take = lambda ref, b: ref[pl.ds(b + r, qb, stride=dil), :]
            if s == 0:
                k_prev, v_prev = take(kp_ref, 0), take(vp_ref, 0)
            else:
                k_prev, v_prev = take(kc_ref, base - win), take(vc_ref, base - win)
            k2 = jnp.concatenate([k_prev, take(kc_ref, base)], axis=0)
            v2 = jnp.concatenate([v_prev, take(vc_ref, base)], axis=0)
            k2r = pltpu.roll(k2, HEAD_DIM, 1)
            v2r = pltpu.roll(v2, HEAD_DIM, 1)
            lse_tile = lse_ref[pl.ds(base + r, qb, stride=dil), :]
            for c in range(kv_per_step):
                kd = (jnp.where(lo2, k2, k2r) if c == 0 else jnp.where(lo2, k2r, k2)).astype(BF16)
                vd = (jnp.where(lo2, v2, v2r) if c == 0 else jnp.where(lo2, v2r, v2)).astype(BF16)
                parts = []
                for u in range(cols_per_kv):
                    col = c * cols_per_kv + u
                    qc = q_ref[col, pl.ds(base + r, qb, stride=dil), :]
                    parts.append(jnp.where(lo, qc, 0.0))
                    parts.append(jnp.where(lo, 0.0, qc))
                q4 = jnp.concatenate(parts, axis=0).astype(BF16)
                sc = lax.dot_general(q4, kd, _NT, preferred_element_type=F32) * scale
                sc = jnp.where(valid, sc, NEG_INF)
                m = jnp.max(sc, axis=1, keepdims=True)
                p = jnp.exp(sc - m)
                l = jnp.sum(p, axis=1, keepdims=True)
                o4 = jnp.dot(p.astype(BF16), vd, preferred_element_type=F32) / l
                lse4 = m + jnp.log(l)
                for u in range(cols_per_kv):
                    col = c * cols_per_kv + u
                    ra = (2 * u) * qb
                    rb = (2 * u + 1) * qb
                    o_ref[col, pl.ds(base + r, qb, stride=dil), :] = jnp.where(lo, o4[ra:ra + qb], o4[rb:rb + qb])
                    head = pair * (2 * ratio) + 2 * col
                    lse_tile = jnp.where(lane == head, lse4[ra:ra + qb], lse_tile)
                    lse_tile = jnp.where(lane == head + 1, lse4[rb:rb + qb], lse_tile)
            lse_ref[pl.ds(base + r, qb, stride=dil), :] = lse_tile
            return carry

        if dil == 1:
            residue(0, 0)
        else:
            lax.fori_loop(0, dil, residue, 0)


def band_attention(q, kv, g, batch, seq):
    window, dil = WINDOWS[g], DILATIONS[g]
    qb = window // dil
    assert seq % window == 0
    nblk = max(1, 512 // window)
    step_rows = nblk * window
    assert seq % step_rows == 0
    steps = seq // step_rows
    wins = seq // window
    q_planes = q.shape[0] // N_GROUPS
    kv_planes = kv.shape[0] // (2 * N_GROUPS)
    q_blk = q_planes // kv_planes
    kk = g * kv_planes
    kvv = (N_GROUPS + g) * kv_planes
    prev_row = lambda b, n: b * wins + jnp.maximum(n * nblk - 1, 0)
    o, lse = pl.pallas_call(
        functools.partial(_band_attn_body, qb=qb, dil=dil, nblk=nblk, ratio=KV_RATIO),
        grid=(batch, steps, kv_planes),
        in_specs=[pl.BlockSpec((q_blk, step_rows, LANES), lambda b, n, p: (g * kv_planes + p, b * steps + n, 0)),
                  pl.BlockSpec((None, step_rows, LANES), lambda b, n, p: (kk + p, b * steps + n, 0)),
                  pl.BlockSpec((None, window, LANES), lambda b, n, p: (kk + p, prev_row(b, n), 0)),
                  pl.BlockSpec((None, step_rows, LANES), lambda b, n, p: (kvv + p, b * steps + n, 0)),
                  pl.BlockSpec((None, window, LANES), lambda b, n, p: (kvv + p, prev_row(b, n), 0))],
        out_specs=[pl.BlockSpec((q_blk, step_rows, LANES), lambda b, n, p: (p, b * steps + n, 0)),
                   pl.BlockSpec((step_rows, LANES), lambda b, n, p: (b * steps + n, 0))],
        out_shape=[jax.ShapeDtypeStruct((q_planes, batch * seq, LANES), F32),
                   jax.ShapeDtypeStruct((batch * seq, LANES), F32)],
        compiler_params=_cparams(("parallel", "parallel", "arbitrary")),
        name="band_attention",
    )(q, kv, kv, kv, kv)
    return o, lse


def _gather_attn_body(q_ref, kc_ref, vc_ref, kn_ref, vn_ref, o_ref, lse_ref, *, window, dil, wb, t_new, n_kv,
                      precise):
    t = pl.program_id(1)
    r = (wb + t) % dil
    scale = HEAD_DIM ** -0.5
    q = q_ref[0, 0]
    n_heads = q.shape[0]
    s_c = _dot(q, kc_ref[0], _NT, precise) * scale
    s_n = _dot(q, kn_ref[0], _NT, precise) * scale
    mi = lax.broadcasted_iota(jnp.int32, s_c.shape, 1)
    valid_c = (mi * dil + r) >= (wb + t - window)
    ti = lax.broadcasted_iota(jnp.int32, s_n.shape, 1)
    valid_n = (ti <= t) & (lax.rem(jnp.maximum(t - ti, 0), dil) == 0) & (ti < t_new)
    s_c = jnp.where(valid_c, s_c, NEG_INF)
    s_n = jnp.where(valid_n, s_n, NEG_INF)
    m = jnp.maximum(jnp.max(s_c, axis=1, keepdims=True), jnp.max(s_n, axis=1, keepdims=True))
    p_c = jnp.exp(s_c - m)
    p_n = jnp.exp(s_n - m)
    l = jnp.sum(p_c, axis=1, keepdims=True) + jnp.sum(p_n, axis=1, keepdims=True)
    o = (_dot(p_c, vc_ref[0], _NN, precise) + _dot(p_n, vn_ref[0], _NN, precise)) / l
    col_kv = lax.broadcasted_iota(jnp.int32, o.shape, 1) // HEAD_DIM
    row_kv = lax.broadcasted_iota(jnp.int32, o.shape, 0) // (n_heads // n_kv)
    o = jnp.where(col_kv == row_kv, o, 0.0)
    acc = o[:, 0:HEAD_DIM]
    for kh in range(1, n_kv):
        acc = acc + o[:, kh * HEAD_DIM:(kh + 1) * HEAD_DIM]
    o_ref[0, 0] = acc
    lse_ref[0, 0] = jnp.broadcast_to(m + jnp.log(l), lse_ref.shape[2:])


def gather_attention(q_exp, cache, k_new, v_new, g, t_new, precise):
    window, dil = WINDOWS[g], DILATIONS[g]
    batch, wb = cache.shape[0], cache.shape[1]
    n_kv = cache.shape[3]
    kvd = n_kv * HEAD_DIM
    assert wb % dil == 0
    rows = wb // dil
    n_heads = q_exp.shape[2]
    t_pad = k_new.shape[1]
    cv = cache.reshape(batch, rows, dil * 2 * kvd)
    o, lse = pl.pallas_call(
        functools.partial(_gather_attn_body, window=window, dil=dil, wb=wb, t_new=t_new, n_kv=n_kv,
                          precise=precise),
        grid=(batch, t_new),
        in_specs=[pl.BlockSpec((1, 1, n_heads, kvd), lambda b, t: (b, t, 0, 0)),
                  pl.BlockSpec((1, rows, kvd), lambda b, t: (b, 0, ((wb + t) % dil) * 2)),
                  pl.BlockSpec((1, rows, kvd), lambda b, t: (b, 0, ((wb + t) % dil) * 2 + 1)),
                  pl.BlockSpec((1, t_pad, kvd), lambda b, t: (b, 0, 0)),
                  pl.BlockSpec((1, t_pad, kvd), lambda b, t: (b, 0, 0))],
        out_specs=[pl.BlockSpec((1, 1, n_heads, HEAD_DIM), lambda b, t: (b, t, 0, 0)),
                   pl.BlockSpec((1, 1, n_heads, LANES), lambda b, t: (b, t, 0, 0))],
        out_shape=[jax.ShapeDtypeStruct((batch, t_new, n_heads, HEAD_DIM), F32),
                   jax.ShapeDtypeStruct((batch, t_new, n_heads, LANES), F32)],
        compiler_params=_cparams(("parallel", "arbitrary")),
        name="gather_attention",
    )(q_exp, cv, cv, k_new, v_new)
    lse = jnp.pad(lse[..., 0].reshape(batch * t_new, n_heads), ((0, 0), (0, LANES - n_heads)))
    return o.reshape(batch * t_new, n_heads * HEAD_DIM), lse


def _merge_body(o0_ref, o1_ref, o2_ref, l0_ref, l1_ref, l2_ref, e_ref, out_ref):
    l0, l1, l2 = l0_ref[...], l1_ref[...], l2_ref[...]
    mx = jnp.maximum(jnp.maximum(l0, l1), l2)
    e0, e1, e2 = jnp.exp(l0 - mx), jnp.exp(l1 - mx), jnp.exp(l2 - mx)
    inv = 1.0 / (e0 + e1 + e2)
    w_exps = []
    for e in (e0, e1, e2):
        wh, wl = _split_bf16(e * inv, 2)
        w_exps.append(jnp.dot(wh, e_ref[...], preferred_element_type=F32)
                      + jnp.dot(wl, e_ref[...], preferred_element_type=F32))
    for c in range(o0_ref.shape[0]):
        cs = slice(c * LANES, (c + 1) * LANES)
        acc = (w_exps[0][:, cs] * o0_ref[c] + w_exps[1][:, cs] * o1_ref[c] + w_exps[2][:, cs] * o2_ref[c])
        out_ref[:, cs] = acc.astype(out_ref.dtype)


def merge_groups(outs, lses, tm, out_dtype):
    planes, m, _ = outs[0].shape
    d_in = planes * LANES
    assert d_in // HEAD_DIM <= LANES
    expand = (jnp.arange(LANES)[:, None] == (jnp.arange(d_in) // HEAD_DIM)[None, :]).astype(BF16)
    o_spec = pl.BlockSpec((planes, tm, LANES), lambda i: (0, i, 0))
    l_spec = pl.BlockSpec((tm, LANES), lambda i: (i, 0))
    return pl.pallas_call(
        _merge_body,
        grid=(m // tm,),
        in_specs=[o_spec, o_spec, o_spec, l_spec, l_spec, l_spec,
                  pl.BlockSpec((LANES, d_in), lambda i: (0, 0))],
        out_specs=pl.BlockSpec((tm, d_in), lambda i: (i, 0)),
        out_shape=jax.ShapeDtypeStruct((m, d_in), out_dtype),
        compiler_params=_cparams(("parallel",)),
        name="merge_groups",
    )(*outs, *lses, expand)


def _gateup_body(h_ref, wg_ref, wu_ref, o_ref, wgbf_ref, wubf_ref, *, precise):
    h = h_ref[...]
    g = _dot(h, _weight_tile(wg_ref, wgbf_ref, precise), precise=precise)
    u = _dot(h, _weight_tile(wu_ref, wubf_ref, precise), precise=precise)
    o_ref[...] = (g * _sigmoid(g) * u).astype(o_ref.dtype)


def gate_up(h, w_gate, w_up, layer, tm, tn=512, precise=False):
    m, d = h.shape
    f = w_gate.shape[2]
    w_spec = pl.BlockSpec((None, d, tn), lambda j, i: (layer, 0, j))
    return pl.pallas_call(
        functools.partial(_gateup_body, precise=precise),
        grid=(f // tn, m // tm),
        in_specs=[pl.BlockSpec((tm, d), lambda j, i: (i, 0)), w_spec, w_spec],
        out_specs=pl.BlockSpec((tm, tn), lambda j, i: (i, j)),
        out_shape=jax.ShapeDtypeStruct((m, f), F32 if precise else BF16),
        scratch_shapes=[_weight_scratch(d, tn, precise), _weight_scratch(d, tn, precise)],
        compiler_params=_cparams(("parallel", "arbitrary")),
        name="gate_up",
    )(h, w_gate, w_up)


def _gather_rows_body(tok_ref, src_hbm, o_ref, buf, sem, *, tm):
    i = pl.program_id(0)
    n = pl.num_programs(0)

    def issue(step, slot):
        def body(r, carry):
            tok = tok_ref[step * tm + r]
            pltpu.make_async_copy(src_hbm.at[pl.ds(tok, 1)], buf.at[slot, pl.ds(r, 1)], sem.at[slot]).start()
            return carry
        lax.fori_loop(0, tm, body, 0)

    @pl.when(i == 0)
    def _():
        issue(0, 0)

    slot = i % 2

    @pl.when(i + 1 < n)
    def _():
        issue(i + 1, 1 - slot)

    pltpu.make_async_copy(src_hbm.at[pl.ds(0, tm)], buf.at[slot], sem.at[slot]).wait()
    o_ref[...] = buf[slot].astype(o_ref.dtype)


def gather_rows(tok, src, tm, out_dtype):
    n_slots = tok.shape[0]
    rows, d = src.shape
    assert n_slots % tm == 0 and rows >= tm
    grid_spec = pltpu.PrefetchScalarGridSpec(
        num_scalar_prefetch=1,
        grid=(n_slots // tm,),
        in_specs=[pl.BlockSpec(memory_space=pl.ANY)],
        out_specs=pl.BlockSpec((tm, d), lambda i, tok: (i, 0)),
        scratch_shapes=[pltpu.VMEM((2, tm, d), src.dtype), pltpu.SemaphoreType.DMA((2,))])
    return pl.pallas_call(
        functools.partial(_gather_rows_body, tm=tm),
        grid_spec=grid_spec,
        out_shape=jax.ShapeDtypeStruct((n_slots, d), out_dtype),
        compiler_params=_cparams(("arbitrary",)),
        name="gather_rows",
    )(tok, src)


def _expert_changed(te_ref):
    i = pl.program_id(1)
    return (i == 0) | (te_ref[i] != te_ref[jnp.maximum(i - 1, 0)])


def _moe_gateup_body(te_ref, na_ref, x_ref, wg_ref, wu_ref, o_ref, wgbf_ref, wubf_ref):
    i = pl.program_id(1)

    @pl.when(_expert_changed(te_ref))
    def _():
        wgbf_ref[...] = wg_ref[...].astype(BF16)
        wubf_ref[...] = wu_ref[...].astype(BF16)

    @pl.when(i < na_ref[0])
    def _():
        x = x_ref[...]
        g = jnp.dot(x, wgbf_ref[...], preferred_element_type=F32)
        u = jnp.dot(x, wubf_ref[...], preferred_element_type=F32)
        o_ref[...] = (g * _sigmoid(g) * u).astype(o_ref.dtype)

    @pl.when(i >= na_ref[0])
    def _():
        o_ref[...] = jnp.zeros_like(o_ref)


def moe_gate_up(tile_expert, n_active, xs, w_gate, w_up, layer, tm, tn=512):
    npad, d = xs.shape
    f = w_gate.shape[3]
    w_spec = pl.BlockSpec((None, None, d, tn), lambda j, i, te, na: (layer, te[i], 0, j))
    grid_spec = pltpu.PrefetchScalarGridSpec(
        num_scalar_prefetch=2,
        grid=(f // tn, npad // tm),
        in_specs=[pl.BlockSpec((tm, d), lambda j, i, te, na: (i, 0)), w_spec, w_spec],
        out_specs=pl.BlockSpec((tm, tn), lambda j, i, te, na: (i, j)),
        scratch_shapes=[pltpu.VMEM((d, tn), BF16), pltpu.VMEM((d, tn), BF16)])
    return pl.pallas_call(
        _moe_gateup_body,
        grid_spec=grid_spec,
        out_shape=jax.ShapeDtypeStruct((npad, f), BF16),
        compiler_params=_cparams(("arbitrary", "arbitrary")),
        name="moe_gate_up",
    )(tile_expert, n_active, xs, w_gate, w_up)


def _moe_down_body(te_ref, na_ref, a_ref, w_ref, gate_ref, o_ref, wbf_ref):
    i = pl.program_id(1)

    @pl.when(_expert_changed(te_ref))
    def _():
        wbf_ref[...] = w_ref[...].astype(BF16)

    @pl.when(i < na_ref[0])
    def _():
        o_ref[...] = jnp.dot(a_ref[...], wbf_ref[...], preferred_element_type=F32) * gate_ref[...]

    @pl.when(i >= na_ref[0])
    def _():
        o_ref[...] = jnp.zeros_like(o_ref)


def moe_down(tile_expert, n_active, a, w_down, slot_gate, layer, tm, tn=512):
    npad, f = a.shape
    d = w_down.shape[3]
    grid_spec = pltpu.PrefetchScalarGridSpec(
        num_scalar_prefetch=2,
        grid=(d // tn, npad // tm),
        in_specs=[pl.BlockSpec((tm, f), lambda j, i, te, na: (i, 0)),
                  pl.BlockSpec((None, None, f, tn), lambda j, i, te, na: (layer, te[i], 0, j)),
                  pl.BlockSpec((tm, 1), lambda j, i, te, na: (i, 0))],
        out_specs=pl.BlockSpec((tm, tn), lambda j, i, te, na: (i, j)),
        scratch_shapes=[pltpu.VMEM((f, tn), BF16)])
    return pl.pallas_call(
        _moe_down_body,
        grid_spec=grid_spec,
        out_shape=jax.ShapeDtypeStruct((npad, d), F32),
        compiler_params=_cparams(("arbitrary", "arbitrary")),
        name="moe_down",
    )(tile_expert, n_active, a, w_down, slot_gate)


def moe_plan(idx, gates, tm, n_tiles):
    n_tok = idx.shape[0]
    e_flat = idx.reshape(-1)
    order = jnp.argsort(e_flat, stable=True).astype(jnp.int32)
    onehot = (e_flat[:, None] == jnp.arange(N_EXPERTS)[None, :]).astype(jnp.int32)
    counts = jnp.sum(onehot, axis=0)
    within_expert = jnp.sum((jnp.cumsum(onehot, axis=0) - onehot) * onehot, axis=1)
    tiles_per = (counts + tm - 1) // tm
    tile_end = jnp.cumsum(tiles_per)
    n_active = tile_end[-1]
    slot_start = (tile_end - tiles_per) * tm
    sorted_start = jnp.cumsum(counts) - counts
    tile_ids = jnp.minimum(jnp.arange(n_tiles), n_active - 1)
    tile_expert = jnp.sum((tile_ids[:, None] >= tile_end[None, :]).astype(jnp.int32), axis=1)
    slot = jnp.arange(n_tiles * tm)
    slot_e = jnp.repeat(tile_expert, tm)
    within = slot - slot_start[slot_e]
    slot_ok = (within < counts[slot_e]) & (slot < n_active * tm)
    src = order[jnp.clip(sorted_start[slot_e] + within, 0, 2 * n_tok - 1)]
    slot_token = jnp.where(slot_ok, src // 2, 0)
    slot_gate = jnp.where(slot_ok, gates.reshape(-1)[src], 0.0)
    pos = slot_start[e_flat] + within_expert
    return (tile_expert.astype(jnp.int32), n_active.reshape(1).astype(jnp.int32),
            slot_token, slot_gate.reshape(-1, 1).astype(F32), pos.reshape(n_tok, 2))


def _softplus(x):
    return jnp.maximum(x, 0.0) + jnp.log(1.0 + jnp.exp(-jnp.abs(x)))


def _ssd_body(x_ref, bm_ref, cm_ref, z_ref, dt_ref, dtt_ref,
              cwx_ref, cwb_ref, cwc_ref, cbx_ref, cbb_ref, cbc_ref,
              dtb_ref, dtbt_ref, al_ref, alt_ref, dsk_ref, nw_ref,
              cix_ref, cib_ref, cic_ref, h0_ref, eexp_ref, eall_ref,
              g_ref, hf_ref,
              xbuf, bbuf, cbuf, state, *, q, valid, hpg, precise):
    c = pl.program_id(2)
    nc = pl.num_programs(2)
    halo = SUBLANES

    @pl.when(c == 0)
    def _():
        xbuf[0:halo] = cix_ref[0]
        bbuf[0:halo] = cib_ref[0]
        cbuf[0:halo] = cic_ref[0]
        state[...] = h0_ref[0].T

    xbuf[halo:halo + q] = x_ref[0]
    bbuf[halo:halo + q] = bm_ref[0]
    cbuf[halo:halo + q] = cm_ref[0]

    def conv_silu(buf, cw_ref, cb_ref):
        acc = cb_ref[...]
        for j in range(CONV_W):
            lo = halo - (CONV_W - 1) + j
            acc = acc + buf[lo:lo + q] * cw_ref[j:j + 1]
        return acc * _sigmoid(acc)

    xs = conv_silu(xbuf, cwx_ref, cbx_ref)
    bmat = conv_silu(bbuf, cwb_ref, cbb_ref)
    cmat = conv_silu(cbuf, cwc_ref, cbc_ref)
    xbuf[0:halo] = xbuf[q:q + halo]
    bbuf[0:halo] = bbuf[q:q + halo]
    cbuf[0:halo] = cbuf[q:q + halo]

    dt = _softplus(dt_ref[0] + dtb_ref[...])
    dtt = _softplus(dtt_ref[0] + dtbt_ref[...])
    if valid < q:
        dt = jnp.where(lax.broadcasted_iota(jnp.int32, dt.shape, 0) < valid, dt, 0.0)
        dtt = jnp.where(lax.broadcasted_iota(jnp.int32, dtt.shape, 1) < valid, dtt, 0.0)
    a = dt * (-jnp.exp(al_ref[...]))
    at = dtt * (-jnp.exp(alt_ref[...]))
    ii = lax.broadcasted_iota(jnp.int32, (q, q), 0)
    jj = lax.broadcasted_iota(jnp.int32, (q, q), 1)
    lower = ii >= jj
    tril = lower.astype(BF16)
    triu = (ii <= jj).astype(BF16)
    a_cum = sum(jnp.dot(tril, part, preferred_element_type=F32) for part in _split_bf16(a, 3))
    a_cumt = sum(jnp.dot(part, triu, preferred_element_type=F32) for part in _split_bf16(at, 3))
    a_last = a_cum[q - 1:q, :]
    e_cum = jnp.exp(a_cum)
    e_end = jnp.exp(a_last - a_cum)

    expand = lambda v, e_ref: sum(jnp.dot(part, e_ref[...], preferred_element_type=F32)
                                  for part in _split_bf16(v, 3))
    per_lane = expand(jnp.concatenate([dt, e_cum, e_end], axis=0), eexp_ref)
    dt_x, ecum_x, eend_x = per_lane[0:q], per_lane[q:2 * q], per_lane[2 * q:3 * q]
    col_all = expand(a_cum, eall_ref)

    xc = xs * dt_x
    xw = xc * eend_x
    if not precise:
        bmat = bmat.astype(BF16)
        cmat = cmat.astype(BF16)
        xc = xc.astype(BF16)
    st = state[...]
    cb = _dot(cmat, bmat, _NT, precise)
    y = _dot(cmat, st, _NN, precise) * ecum_x + dsk_ref[...] * xs

    lo = lax.broadcasted_iota(jnp.int32, (q, LANES), 1) < SSM_HEAD_DIM
    pairs = []
    for u in range(hpg // 2):
        xcp = xc[:, u * LANES:(u + 1) * LANES]
        y2 = []
        for h in (2 * u, 2 * u + 1):
            seg = col_all[:, h * q:(h + 1) * q] - a_cumt[h:h + 1, :]
            y2.append(_dot(cb * jnp.exp(jnp.where(lower, seg, NEG_INF)), xcp, _NN, precise))
        pairs.append(jnp.where(lo, y2[0], y2[1]))
    y = y + jnp.concatenate(pairs, axis=1)
    state[...] = st * ecum_x[q - 1:q, :] + _dot(bmat, xw, _TN, precise)

    z = z_ref[0]
    gz = y * (z * _sigmoid(z))
    ms = jnp.mean(gz * gz, axis=1, keepdims=True)
    g_ref[0] = (gz * lax.rsqrt(ms + EPS) * nw_ref[...]).astype(g_ref.dtype)

    @pl.when(c == nc - 1)
    def _():
        hf_ref[0] = state[...].T


def ssd_core(z, xbc, dt_raw, conv_init, h0, conv_w, conv_b, dt_bias, a_log, d_skip, norm_w, *, batch, seq, q, valid,
             precise):
    d_inner = z.shape[1]
    n_heads = dt_raw.shape[1]
    hpg = n_heads // SSM_GROUPS
    gw = hpg * SSM_HEAD_DIM
    nst = D_STATE
    assert seq % q == 0 and gw % LANES == 0
    nc = seq // q
    xb_blk = d_inner // nst
    cm_blk = xb_blk + SSM_GROUPS
    z3 = z.reshape(batch, seq, d_inner)
    xbc3 = xbc.reshape(batch, seq, xbc.shape[1])
    hp2 = 2 * hpg
    assert q == LANES and 2 * SSM_HEAD_DIM == LANES
    dt4 = jnp.pad(dt_raw.reshape(batch, seq, SSM_GROUPS, hpg).transpose(0, 2, 1, 3),
                  ((0, 0), (0, 0), (0, 0), (0, hpg)))
    dtt4 = dt4.transpose(0, 1, 3, 2)
    pad_cols = lambda v: jnp.pad(v.reshape(SSM_GROUPS, 1, hpg), ((0, 0), (0, 0), (0, hpg)))
    pad_rows = lambda v: jnp.pad(v.reshape(SSM_GROUPS, hpg, 1), ((0, 0), (0, hpg), (0, 0)))
    dsk = jnp.repeat(d_skip, SSM_HEAD_DIM).reshape(1, d_inner)
    cb2 = conv_b.reshape(1, -1)
    nw = norm_w.reshape(1, d_inner)
    head = jnp.arange(hp2)[:, None]
    e_exp = (head == (jnp.arange(gw) // SSM_HEAD_DIM)[None, :]).astype(BF16)
    e_all = (head == (jnp.arange(hpg * q) // q)[None, :]).astype(BF16)

    grp = lambda b, g, c: (b, c, g)
    in_specs = [
        pl.BlockSpec((1, q, gw), grp),
        pl.BlockSpec((1, q, nst), lambda b, g, c: (b, c, xb_blk + g)),
        pl.BlockSpec((1, q, nst), lambda b, g, c: (b, c, cm_blk + g)),
        pl.BlockSpec((1, q, gw), grp),
        pl.BlockSpec((1, None, q, hp2), lambda b, g, c: (b, g, c, 0)),
        pl.BlockSpec((1, None, hp2, q), lambda b, g, c: (b, g, 0, c)),
        pl.BlockSpec((CONV_W, gw), lambda b, g, c: (0, g)),
        pl.BlockSpec((CONV_W, nst), lambda b, g, c: (0, xb_blk + g)),
        pl.BlockSpec((CONV_W, nst), lambda b, g, c: (0, cm_blk + g)),
        pl.BlockSpec((1, gw), lambda b, g, c: (0, g)),
        pl.BlockSpec((1, nst), lambda b, g, c: (0, xb_blk + g)),
        pl.BlockSpec((1, nst), lambda b, g, c: (0, cm_blk + g)),
        pl.BlockSpec((None, 1, hp2), lambda b, g, c: (g, 0, 0)),
        pl.BlockSpec((None, hp2, 1), lambda b, g, c: (g, 0, 0)),
        pl.BlockSpec((None, 1, hp2), lambda b, g, c: (g, 0, 0)),
        pl.BlockSpec((None, hp2, 1), lambda b, g, c: (g, 0, 0)),
        pl.BlockSpec((1, gw), lambda b, g, c: (0, g)),
        pl.BlockSpec((1, gw), lambda b, g, c: (0, g)),
        pl.BlockSpec((1, SUBLANES, gw), lambda b, g, c: (b, 0, g)),
        pl.BlockSpec((1, SUBLANES, nst), lambda b, g, c: (b, 0, xb_blk + g)),
        pl.BlockSpec((1, SUBLANES, nst), lambda b, g, c: (b, 0, cm_blk + g)),
        pl.BlockSpec((1, gw, nst), lambda b, g, c: (b, g, 0)),
        pl.BlockSpec((hp2, gw), lambda b, g, c: (0, 0)),
        pl.BlockSpec((hp2, hpg * q), lambda b, g, c: (0, 0)),
    ]
    g_out, h_fin = pl.pallas_call(
        functools.partial(_ssd_body, q=q, valid=valid, hpg=hpg, precise=precise),
        grid=(batch, SSM_GROUPS, nc),
        in_specs=in_specs,
        out_specs=[pl.BlockSpec((1, q, gw), grp),
                   pl.BlockSpec((1, gw, nst), lambda b, g, c: (b, g, 0))],
        out_shape=[jax.ShapeDtypeStruct((batch, seq, d_inner), F32 if precise else BF16),
                   jax.ShapeDtypeStruct(h0.shape, F32)],
        scratch_shapes=[pltpu.VMEM((q + SUBLANES, gw), F32),
                        pltpu.VMEM((q + SUBLANES, nst), F32),
                        pltpu.VMEM((q + SUBLANES, nst), F32),
                        pltpu.VMEM((nst, gw), F32)],
        compiler_params=_cparams(("parallel", "parallel", "arbitrary")),
        name="ssd_core",
    )(xbc3, xbc3, xbc3, z3, dt4, dtt4,
      conv_w, conv_w, conv_w, cb2, cb2, cb2,
      pad_cols(dt_bias), pad_rows(dt_bias), pad_cols(a_log), pad_rows(a_log), dsk, nw,
      conv_init, conv_init, conv_init, h0, e_exp, e_all)
    return g_out.reshape(batch * seq, d_inner), h_fin


def _attn_tokens(h, w_qkv, layer, q_gain, k_gain, tables, tm, table_period_rows, precise):
    q_cols = (w_qkv.shape[2] * KV_RATIO) // (KV_RATIO + 2)
    kv_cols = (w_qkv.shape[2] - q_cols) // 2
    q = proj_normrope(h, w_qkv, layer, q_gain, tables, col_off=0, n=q_cols, n_rope_cols=q_cols,
                      tm=tm, table_period_rows=table_period_rows, precise=precise)
    kv = proj_normrope(h, w_qkv, layer, k_gain, tables, col_off=q_cols, n=2 * kv_cols, n_rope_cols=kv_cols,
                       tm=tm, table_period_rows=table_period_rows, precise=precise)
    return q, kv


def _kv_window(kv, g, batch, seq, keep):
    per = kv.shape[0] // (2 * N_GROUPS)
    kv4 = kv.reshape(2 * N_GROUPS, per, batch, seq, LANES)[:, :, :, seq - keep:]
    kv_g = jnp.stack([kv4[g], kv4[N_GROUPS + g]], axis=0)
    return kv_g.transpose(2, 3, 0, 1, 4).reshape(batch, keep, 2, per * LANES // HEAD_DIM, HEAD_DIM)


def _planes_to_rows(x):
    return x.transpose(1, 0, 2).reshape(x.shape[1], x.shape[0] * LANES)


def _rows_to_planes(x):
    return x.reshape(x.shape[0], x.shape[1] // LANES, LANES).transpose(1, 0, 2)


def kernel(x_prompt, x_sample, cache_kv_w128, cache_kv_w512, cache_kv_w2048, state_conv, state_ssm,
           norm_mix, norm_ffn, attn_w_qkv, attn_q_norm, attn_k_norm, attn_w_o,
           ssm_w_in, ssm_conv_w, ssm_conv_b, ssm_dt_bias, ssm_a_log, ssm_d, ssm_norm, ssm_w_out,
           ffn_w_gate, ffn_w_up, ffn_w_down, moe_w_router, moe_w_gate, moe_w_up, moe_w_down):
    bp, sp, d_model = x_prompt.shape
    bs, ts, _ = x_sample.shape
    depth = norm_mix.shape[0]
    mp, ms = bp * sp, bs * ts
    past_len_caches = (cache_kv_w128, cache_kv_w512, cache_kv_w2048)
    tmp = 1024
    tmp_deep = 512
    tms = ms
    tm_moe = 512
    n_heads = attn_w_o.shape[1] // HEAD_DIM
    n_kv = n_heads // KV_RATIO
    kvd = n_kv * HEAD_DIM
    d_inner = ssm_w_out.shape[1]
    ssm_heads = ssm_dt_bias.shape[1]
    conv_dim = ssm_conv_w.shape[2]
    t_pad = 16

    xp = x_prompt.reshape(mp, d_model)
    xs = x_sample.reshape(ms, d_model)

    tab_p = rope_tables(jnp.arange(sp))
    tab_s = rope_tables(PAST_LEN + (jnp.arange(ms) % ts))

    kv_p = [[] for _ in range(N_GROUPS)]
    kv_s = [[] for _ in range(N_GROUPS)]
    conv_p, conv_s, ssm_p, ssm_s = [], [], [], []

    for i in range(depth):
        j = i // 2
        hp = rmsnorm(xp, norm_mix[i], tmp)
        hs = rmsnorm(xs, norm_mix[i], tms, out_dtype=F32)
        if i % 2 == 0:
            q, kv = _attn_tokens(hp, attn_w_qkv, j, attn_q_norm[j], attn_k_norm[j], tab_p, tmp, sp, False)
            outs, lses = [], []
            for g in range(N_GROUPS):
                o, l = band_attention(q, kv, g, bp, sp)
                outs.append(o)
                lses.append(l)
                kv_p[g].append(_kv_window(kv, g, bp, sp, min(WINDOWS[g], sp)))
            xp = matmul(merge_groups(outs, lses, 256, BF16), attn_w_o, j, residual=xp, tm=tmp, tn=512)
            q, kv = _attn_tokens(hs, attn_w_qkv, j, attn_q_norm[j], attn_k_norm[j], tab_s, tms, ms, True)
            q6 = _planes_to_rows(q).reshape(bs, ts, N_GROUPS, n_kv, KV_RATIO, HEAD_DIM)
            kv5 = _planes_to_rows(kv).reshape(bs, ts, 2 * N_GROUPS, kvd)
            eye = jnp.eye(n_kv, dtype=q.dtype)
            outs, lses = [], []
            for g in range(N_GROUPS):
                q_exp = (q6[:, :, g, :, :, None, :] * eye[None, None, :, None, :, None]).reshape(
                    bs, ts, n_heads, kvd)
                k_new = jnp.pad(kv5[:, :, g], ((0, 0), (0, t_pad - ts), (0, 0)))
                v_new = jnp.pad(kv5[:, :, N_GROUPS + g], ((0, 0), (0, t_pad - ts), (0, 0)))
                cache = past_len_caches[g][j]
                o, l = gather_attention(q_exp, cache, k_new, v_new, g, ts, True)
                outs.append(_rows_to_planes(o))
                lses.append(l)
                wb = cache.shape[1]
                new_rows = jnp.stack([kv5[:, :, g], kv5[:, :, N_GROUPS + g]], axis=2).reshape(
                    bs, ts, 2, n_kv, HEAD_DIM)
                kv_s[g].append(jnp.concatenate([cache, new_rows], axis=1)[:, -wb:])
            xs = matmul(merge_groups(outs, lses, tms, F32), attn_w_o, j, residual=xs, tm=tms, tn=512,
                        precise=True)
            hp = rmsnorm(xp, norm_ffn[i], tmp)
            hs = rmsnorm(xs, norm_ffn[i], tms, out_dtype=F32)
            xp = matmul(gate_up(hp, ffn_w_gate, ffn_w_up, j, tmp), ffn_w_down, j, residual=xp, tm=tmp_deep,
                        tn=256)
            xs = matmul(gate_up(hs, ffn_w_gate, ffn_w_up, j, tms, precise=True), ffn_w_down, j,
                        residual=xs, tm=tms, tn=256, precise=True)
        else:
            w_dt = ssm_w_in[j:j + 1, :, d_inner + conv_dim:]

            def mixer(h, x, tm, batch, seq_pad, seq, conv_init, h0, precise):
                z = matmul(h, ssm_w_in, j, n=d_inner, col_off=0, tm=tm, tn=512, precise=precise)
                xbc = matmul(h, ssm_w_in, j, n=conv_dim, col_off=d_inner, tm=tm, tn=512, precise=precise)
                dt_raw = matmul(h, w_dt, 0, tm=tm, tn=ssm_heads, precise=precise)
                if seq_pad != seq:
                    padt = lambda v: jnp.pad(v.reshape(batch, seq, -1),
                                             ((0, 0), (0, seq_pad - seq), (0, 0))).reshape(batch * seq_pad, -1)
                    zk, xbck, dtk = padt(z), padt(xbc), padt(dt_raw)
                else:
                    zk, xbck, dtk = z, xbc, dt_raw
                y, h_fin = ssd_core(zk, xbck, dtk, conv_init, h0, ssm_conv_w[j], ssm_conv_b[j], ssm_dt_bias[j],
                                    ssm_a_log[j], ssm_d[j], ssm_norm[j], batch=batch, seq=seq_pad,
                                    q=SSM_CHUNK, valid=min(seq, SSM_CHUNK), precise=precise)
                if seq_pad != seq:
                    y = y.reshape(batch, seq_pad, d_inner)[:, :seq].reshape(batch * seq, d_inner)
                x_new = matmul(y, ssm_w_out, j, residual=x, tm=tm, tn=512, precise=precise)
                return x_new, xbc.reshape(batch, seq, conv_dim), h_fin

            zero_conv = jnp.zeros((bp, SUBLANES, conv_dim), F32)
            zero_h = jnp.zeros((bp, ssm_heads * SSM_HEAD_DIM, D_STATE), F32)
            xp, xbc_p, hfin_p = mixer(hp, xp, tmp, bp, sp, sp, zero_conv, zero_h, False)
            conv_p.append(xbc_p[:, sp - (CONV_W - 1):])
            ssm_p.append(hfin_p.reshape(bp, ssm_heads, SSM_HEAD_DIM, D_STATE))

            prev = state_conv[j]
            conv_init_s = jnp.pad(prev, ((0, 0), (SUBLANES - (CONV_W - 1), 0), (0, 0)))
            h0_s = state_ssm[j].reshape(bs, ssm_heads * SSM_HEAD_DIM, D_STATE)
            xs, xbc_s, hfin_s = mixer(hs, xs, tms, bs, SSM_CHUNK, ts, conv_init_s, h0_s, True)
            conv_s.append(jnp.concatenate([prev, xbc_s], axis=1)[:, -(CONV_W - 1):])
            ssm_s.append(hfin_s.reshape(bs, ssm_heads, SSM_HEAD_DIM, D_STATE))

            hp, gate_p, idx_p = rmsnorm_router(xp, norm_ffn[i], moe_w_router[j], tmp_deep)
            hs, gate_s, idx_s = rmsnorm_router(xs, norm_ffn[i], moe_w_router[j], tms)
            h_all = jnp.concatenate([hp, hs], axis=0)
            gates = jnp.concatenate([gate_p[:, :2], gate_s[:, :2]], axis=0)
            idx = jnp.concatenate([idx_p[:, :2], idx_s[:, :2]], axis=0)
            n_tok = mp + ms
            n_tiles = (2 * n_tok + N_EXPERTS * (tm_moe - 1)) // tm_moe
            tile_expert, n_active, slot_token, slot_gate, pos = moe_plan(idx, gates, tm_moe, n_tiles)
            x_sorted = gather_rows(slot_token, h_all, tm_moe, BF16)
            act = moe_gate_up(tile_expert, n_active, x_sorted, moe_w_gate, moe_w_up, j, tm_moe)
            y_sorted = moe_down(tile_expert, n_active, act, moe_w_down, slot_gate, j, tm_moe, tn=256)
            moe_out = jnp.take(y_sorted, pos[:, 0], axis=0) + jnp.take(y_sorted, pos[:, 1], axis=0)
            xp = xp + moe_out[:mp]
            xs = xs + moe_out[mp:]

    stack = lambda xs_: jnp.stack(xs_, 0)
    return (xp.reshape(bp, sp, d_model), xs.reshape(bs, ts, d_model),
            stack(kv_p[0]), stack(kv_p[1]), stack(kv_p[2]), stack(conv_p), stack(ssm_p),
            stack(kv_s[0]), stack(kv_s[1]), stack(kv_s[2]), stack(conv_s), stack(ssm_s))
```

```python
import functools
import math

import jax
import jax.numpy as jnp
from jax import lax
from jax.experimental import pallas as pl
from jax.experimental.pallas import tpu as pltpu

F32 = jnp.float32
BF16 = jnp.bfloat16

HEAD_DIM = 64
ROPE_DIM = 16
ROPE_THETA = 500000.0
WINDOWS = (128, 512, 2048)
DILATIONS = (1, 4, 16)
N_GROUPS = 3
KV_RATIO = 4
SSM_HEAD_DIM = 64
SSM_GROUPS = 8
D_STATE = 128
CONV_W = 4
SSM_CHUNK = 128
N_EXPERTS = 8
PAST_LEN = 16384
EPS = 1e-6

LANES = 128
SUBLANES = 8
VMEM_LIMIT_BYTES = 56 * 1024 * 1024

NEG_INF = float("-inf")


def _cparams(sem):
    return pltpu.CompilerParams(dimension_semantics=sem, vmem_limit_bytes=VMEM_LIMIT_BYTES)


def _sigmoid(x):
    return 1.0 / (1.0 + jnp.exp(-x))


def _split_bf16(x, parts):
    out = []
    r = x
    for _ in range(parts):
        h = r.astype(BF16)
        out.append(h)
        r = r - h.astype(F32)
    return out


_NN = (((1,), (0,)), ((), ()))
_NT = (((1,), (1,)), ((), ()))
_TN = (((0,), (0,)), ((), ()))


def _dot(a, b, dims=_NN, precise=False):
    mm = lambda x, y: lax.dot_general(x, y, dims, preferred_element_type=F32)
    if not precise:
        return mm(a.astype(BF16), b.astype(BF16))
    ah, al = _split_bf16(a, 2)
    bh, bl = _split_bf16(b, 2)
    return mm(ah, bh) + mm(ah, bl) + mm(al, bh)


def _rmsnorm_body(x_ref, g_ref, o_ref):
    x = x_ref[...]
    ms = jnp.mean(x * x, axis=-1, keepdims=True)
    o_ref[...] = (x * lax.rsqrt(ms + EPS) * g_ref[...]).astype(o_ref.dtype)


def rmsnorm(x, g, tm, out_dtype=None):
    m, d = x.shape
    return pl.pallas_call(
        _rmsnorm_body,
        grid=(m // tm,),
        in_specs=[pl.BlockSpec((tm, d), lambda i: (i, 0)),
                  pl.BlockSpec((1, d), lambda i: (0, 0))],
        out_specs=pl.BlockSpec((tm, d), lambda i: (i, 0)),
        out_shape=jax.ShapeDtypeStruct((m, d), BF16 if out_dtype is None else out_dtype),
        compiler_params=_cparams(("parallel",)),
        name="rmsnorm",
    )(x, g.reshape(1, d))


def _rmsnorm_router_body(x_ref, g_ref, wr_ref, h_ref, gate_ref, idx_ref, *, n_experts):
    x = x_ref[...]
    ms = jnp.mean(x * x, axis=-1, keepdims=True)
    h = x * lax.rsqrt(ms + EPS) * g_ref[...]
    h_ref[...] = h
    hh, hl = _split_bf16(h, 2)
    wh, wl = _split_bf16(wr_ref[...], 2)
    logits = (jnp.dot(hh, wh, preferred_element_type=F32)
              + jnp.dot(hh, wl, preferred_element_type=F32)
              + jnp.dot(hl, wh, preferred_element_type=F32))
    lane = lax.broadcasted_iota(jnp.int32, logits.shape, 1).astype(F32)
    l1 = jnp.where(lane < n_experts, logits, NEG_INF)
    m1 = jnp.max(l1, axis=1, keepdims=True)
    i1 = jnp.min(jnp.where(l1 == m1, lane, float(LANES)), axis=1, keepdims=True)
    l2 = jnp.where(lane == i1, NEG_INF, l1)
    m2 = jnp.max(l2, axis=1, keepdims=True)
    i2 = jnp.min(jnp.where(l2 == m2, lane, float(LANES)), axis=1, keepdims=True)
    e = jnp.exp(m2 - m1)
    g1 = 1.0 / (1.0 + e)
    g2 = e * g1
    gate_ref[...] = jnp.where(lane == 0, g1, jnp.where(lane == 1, g2, 0.0))
    idx_ref[...] = jnp.where(lane == 0, i1, jnp.where(lane == 1, i2, 0.0)).astype(jnp.int32)


def rmsnorm_router(x, g, w_router, tm):
    m, d = x.shape
    n_experts = w_router.shape[1]
    wr = jnp.pad(w_router, ((0, 0), (0, LANES - n_experts)))
    return pl.pallas_call(
        functools.partial(_rmsnorm_router_body, n_experts=n_experts),
        grid=(m // tm,),
        in_specs=[pl.BlockSpec((tm, d), lambda i: (i, 0)),
                  pl.BlockSpec((1, d), lambda i: (0, 0)),
                  pl.BlockSpec((d, LANES), lambda i: (0, 0))],
        out_specs=[pl.BlockSpec((tm, d), lambda i: (i, 0)),
                   pl.BlockSpec((tm, LANES), lambda i: (i, 0)),
                   pl.BlockSpec((tm, LANES), lambda i: (i, 0))],
        out_shape=[jax.ShapeDtypeStruct((m, d), F32),
                   jax.ShapeDtypeStruct((m, LANES), F32),
                   jax.ShapeDtypeStruct((m, LANES), jnp.int32)],
        compiler_params=_cparams(("parallel",)),
        name="rmsnorm_router",
    )(x, g.reshape(1, d), wr)


def _weight_tile(w_ref, wbf_ref, precise):
    if precise:
        return w_ref

    @pl.when(pl.program_id(1) == 0)
    def _():
        wbf_ref[...] = w_ref[...].astype(BF16)

    return wbf_ref


def _weight_scratch(kdim, tn, precise):
    return pltpu.VMEM((2 * SUBLANES, LANES) if precise else (kdim, tn), BF16)


ROW_CHUNK = 256


def _row_chunks(tm):
    step = ROW_CHUNK if tm % ROW_CHUNK == 0 else tm
    return [slice(r, r + step) for r in range(0, tm, step)]


def _mm_body(*refs, has_res, precise):
    if has_res:
        a_ref, w_ref, r_ref, o_ref, wbf_ref = refs
    else:
        a_ref, w_ref, o_ref, wbf_ref = refs
        r_ref = None
    w = _weight_tile(w_ref, wbf_ref, precise)
    for rows in _row_chunks(a_ref.shape[0]):
        acc = _dot(a_ref[rows, :], w[...], precise=precise)
        if has_res:
            acc = acc + r_ref[rows, :]
        o_ref[rows, :] = acc.astype(o_ref.dtype)


def matmul(a, w, layer, *, n=None, col_off=0, k_off=0, kdim=None, residual=None, out_dtype=F32, tm, tn,
           precise=False):
    m = a.shape[0]
    kdim = a.shape[1] if kdim is None else kdim
    n = w.shape[2] if n is None else n
    tn = min(tn, n)
    assert m % tm == 0 and n % tn == 0 and col_off % tn == 0 and k_off % kdim == 0 and w.shape[1] == a.shape[1]
    off = col_off // tn
    kb = k_off // kdim
    in_specs = [pl.BlockSpec((tm, kdim), lambda j, i: (i, kb)),
                pl.BlockSpec((None, kdim, tn), lambda j, i: (layer, kb, j + off))]
    args = [a, w]
    if residual is not None:
        in_specs.append(pl.BlockSpec((tm, tn), lambda j, i: (i, j)))
        args.append(residual)
    return pl.pallas_call(
        functools.partial(_mm_body, has_res=residual is not None, precise=precise),
        grid=(n // tn, m // tm),
        in_specs=in_specs,
        out_specs=pl.BlockSpec((tm, tn), lambda j, i: (i, j)),
        out_shape=jax.ShapeDtypeStruct((m, n), out_dtype),
        scratch_shapes=[_weight_scratch(kdim, tn, precise)],
        compiler_params=_cparams(("parallel", "arbitrary")),
        name="matmul",
    )(*args)


def _proj_normrope_body(h_ref, w_ref, gain_ref, seg_ref, cos_ref, sa_ref, sb_ref, o_ref, wbf_ref, *,
                        n_rope_tiles, precise):
    j = pl.program_id(0)
    w = _weight_tile(w_ref, wbf_ref, precise)
    tn = w.shape[1]
    half = ROPE_DIM // 2

    def product(rows):
        return _dot(h_ref[rows, :], w[...], precise=precise)

    def norm_rope():
        for rows in _row_chunks(h_ref.shape[0]):
            y = product(rows)
            ssq = sum(jnp.dot(part, seg_ref[...], preferred_element_type=F32)
                      for part in _split_bf16(y * y, 2 if precise else 1))
            yn = y * lax.rsqrt(ssq * (1.0 / HEAD_DIM) + EPS) * gain_ref[...]
            c, sa, sb = cos_ref[rows, :], sa_ref[rows, :], sb_ref[rows, :]
            for cc in range(tn // LANES):
                yc = yn[:, cc * LANES:(cc + 1) * LANES]
                o_ref[cc, rows, :] = yc * c + pltpu.roll(yc, LANES - half, 1) * sa + pltpu.roll(yc, half, 1) * sb

    def plain():
        for rows in _row_chunks(h_ref.shape[0]):
            y = product(rows)
            for cc in range(tn // LANES):
                o_ref[cc, rows, :] = y[:, cc * LANES:(cc + 1) * LANES]

    if n_rope_tiles is None:
        norm_rope()
    else:
        pl.when(j < n_rope_tiles)(norm_rope)
        pl.when(j >= n_rope_tiles)(plain)


def proj_normrope(h, w, layer, gain, tables, *, col_off, n, n_rope_cols, tm, table_period_rows, precise):
    m, kdim = h.shape
    tn = 512
    assert n % tn == 0 and col_off % tn == 0 and n_rope_cols % tn == 0 and m % tm == 0
    off = col_off // tn
    n_rope_tiles = None if n_rope_cols == n else n_rope_cols // tn
    head = jnp.arange(tn) // HEAD_DIM
    seg = (head[:, None] == head[None, :]).astype(BF16)
    gain_t = jnp.tile(gain.astype(F32), tn // HEAD_DIM).reshape(1, tn)
    cos_t, sa_t, sb_t = tables
    n_tab_blocks = table_period_rows // tm
    tab_spec = pl.BlockSpec((tm, LANES), lambda j, i: (i % n_tab_blocks, 0))
    return pl.pallas_call(
        functools.partial(_proj_normrope_body, n_rope_tiles=n_rope_tiles, precise=precise),
        grid=(n // tn, m // tm),
        in_specs=[pl.BlockSpec((tm, kdim), lambda j, i: (i, 0)),
                  pl.BlockSpec((None, kdim, tn), lambda j, i: (layer, 0, j + off)),
                  pl.BlockSpec((1, tn), lambda j, i: (0, 0)),
                  pl.BlockSpec((tn, tn), lambda j, i: (0, 0)),
                  tab_spec, tab_spec, tab_spec],
        out_specs=pl.BlockSpec((tn // LANES, tm, LANES), lambda j, i: (j, i, 0)),
        out_shape=jax.ShapeDtypeStruct((n // LANES, m, LANES), F32),
        scratch_shapes=[_weight_scratch(kdim, tn, precise)],
        compiler_params=_cparams(("parallel", "arbitrary")),
        name="proj_normrope",
    )(h, w, gain_t, seg, cos_t, sa_t, sb_t)


def rope_tables(pos):
    half = ROPE_DIM // 2
    inv_freq = jnp.exp(-math.log(ROPE_THETA) * jnp.arange(half, dtype=F32) / half)
    ang = pos.astype(F32)[:, None] * inv_freq[None, :]
    cos, sin = jnp.cos(ang), jnp.sin(ang)
    d = jnp.arange(LANES) % HEAD_DIM
    cos_t = jnp.where(d[None, :] < ROPE_DIM, cos[:, d % half], 1.0)
    sa_t = jnp.where(d[None, :] < half, -sin[:, d % half], 0.0)
    sb_t = jnp.where((d[None, :] >= half) & (d[None, :] < ROPE_DIM), sin[:, d % half], 0.0)
    return cos_t.astype(F32), sa_t.astype(F32), sb_t.astype(F32)


def _band_attn_body(q_ref, kc_ref, kp_ref, vc_ref, vp_ref, o_ref, lse_ref, *, qb, dil, nblk, ratio):
    n = pl.program_id(1)
    pair = pl.program_id(2)
    win = qb * dil
    kv_per_step = LANES // HEAD_DIM
    cols_per_kv = ratio * HEAD_DIM // LANES
    rows = 2 * cols_per_kv * qb
    qi = lax.broadcasted_iota(jnp.int32, (rows, 2 * qb), 0) % qb
    si = lax.broadcasted_iota(jnp.int32, (rows, 2 * qb), 1)
    band = (si >= qi) & (si <= qi + qb)
    band_first = band & ((n > 0) | (si >= qb))
    lane = lax.broadcasted_iota(jnp.int32, (qb, LANES), 1)
    lo = lane < HEAD_DIM
    lo2 = lax.broadcasted_iota(jnp.int32, (2 * qb, LANES), 1) < HEAD_DIM
    scale = HEAD_DIM ** -0.5

    @pl.when(pair == 0)
    def _():
        lse_ref[...] = jnp.zeros_like(lse_ref)

    for s in range(nblk):
        base = s * win
        valid = band_first if s == 0 else band

        def residue(r, carry, base=base, s=s, valid=valid):
            take = lambda ref, b: ref[pl.ds(b + r, qb, stride=dil), :]
            if s == 0:
                k_prev, v_prev = take(kp_ref, 0), take(vp_ref, 0)
            else:
                k_prev, v_prev = take(kc_ref, base - win), take(vc_ref, base - win)
            k2 = jnp.concatenate([k_prev, take(kc_ref, base)], axis=0)
            v2 = jnp.concatenate([v_prev, take(vc_ref, base)], axis=0)
            k2r = pltpu.roll(k2, HEAD_DIM, 1)
            v2r = pltpu.roll(v2, HEAD_DIM, 1)
            lse_tile = lse_ref[pl.ds(base + r, qb, stride=dil), :]
            for c in range(kv_per_step):
                kd = (jnp.where(lo2, k2, k2r) if c == 0 else jnp.where(lo2, k2r, k2)).astype(BF16)
                vd = (jnp.where(lo2, v2, v2r) if c == 0 else jnp.where(lo2, v2r, v2)).astype(BF16)
                parts = []
                for u in range(cols_per_kv):
                    col = c * cols_per_kv + u
                    qc = q_ref[col, pl.ds(base + r, qb, stride=dil), :]
                    parts.append(jnp.where(lo, qc, 0.0))
                    parts.append(jnp.where(lo, 0.0, qc))
                q4 = jnp.concatenate(parts, axis=0).astype(BF16)
                sc = lax.dot_general(q4, kd, _NT, preferred_element_type=F32) * scale
                sc = jnp.where(valid, sc, NEG_INF)
                m = jnp.max(sc, axis=1, keepdims=True)
                p = jnp.exp(sc - m)
                l = jnp.sum(p, axis=1, keepdims=True)
                o4 = jnp.dot(p.astype(BF16), vd, preferred_element_type=F32) / l
                lse4 = m + jnp.log(l)
                for u in range(cols_per_kv):
                    col = c * cols_per_kv + u
                    ra = (2 * u) * qb
                    rb = (2 * u + 1) * qb
                    o_ref[col, pl.ds(base + r, qb, stride=dil), :] = jnp.where(lo, o4[ra:ra + qb], o4[rb:rb + qb])
                    head = pair * (2 * ratio) + 2 * col
                    lse_tile = jnp.where(lane == head, lse4[ra:ra + qb], lse_tile)
                    lse_tile = jnp.where(lane == head + 1, lse4[rb:rb + qb], lse_tile)
            lse_ref[pl.ds(base + r, qb, stride=dil), :] = lse_tile
            return carry

        if dil == 1:
            residue(0, 0)
        else:
            lax.fori_loop(0, dil, residue, 0)


def band_attention(q, kv, g, batch, seq):
    window, dil = WINDOWS[g], DILATIONS[g]
    qb = window // dil
    assert seq % window == 0
    nblk = max(1, 512 // window)
    step_rows = nblk * window
    assert seq % step_rows == 0
    steps = seq // step_rows
    wins = seq // window
    q_planes = q.shape[0] // N_GROUPS
    kv_planes = kv.shape[0] // (2 * N_GROUPS)
    q_blk = q_planes // kv_planes
    kk = g * kv_planes
    kvv = (N_GROUPS + g) * kv_planes
    prev_row = lambda b, n: b * wins + jnp.maximum(n * nblk - 1, 0)
    o, lse = pl.pallas_call(
        functools.partial(_band_attn_body, qb=qb, dil=dil, nblk=nblk, ratio=KV_RATIO),
        grid=(batch, steps, kv_planes),
        in_specs=[pl.BlockSpec((q_blk, step_rows, LANES), lambda b, n, p: (g * kv_planes + p, b * steps + n, 0)),
                  pl.BlockSpec((None, step_rows, LANES), lambda b, n, p: (kk + p, b * steps + n, 0)),
                  pl.BlockSpec((None, window, LANES), lambda b, n, p: (kk + p, prev_row(b, n), 0)),
                  pl.BlockSpec((None, step_rows, LANES), lambda b, n, p: (kvv + p, b * steps + n, 0)),
                  pl.BlockSpec((None, window, LANES), lambda b, n, p: (kvv + p, prev_row(b, n), 0))],
        out_specs=[pl.BlockSpec((q_blk, step_rows, LANES), lambda b, n, p: (p, b * steps + n, 0)),
                   pl.BlockSpec((step_rows, LANES), lambda b, n, p: (b * steps + n, 0))],
        out_shape=[jax.ShapeDtypeStruct((q_planes, batch * seq, LANES), F32),
                   jax.ShapeDtypeStruct((batch * seq, LANES), F32)],
        compiler_params=_cparams(("parallel", "parallel", "arbitrary")),
        name="band_attention",
    )(q, kv, kv, kv, kv)
    return o, lse


def _gather_attn_body(q_ref, kc_ref, vc_ref, kn_ref, vn_ref, o_ref, lse_ref, *, window, dil, wb, t_new, n_kv,
                      precise):
    t = pl.program_id(1)
    r = (wb + t) % dil
    scale = HEAD_DIM ** -0.5
    q = q_ref[0, 0]
    n_heads = q.shape[0]
    s_c = _dot(q, kc_ref[0], _NT, precise) * scale
    s_n = _dot(q, kn_ref[0], _NT, precise) * scale
    mi = lax.broadcasted_iota(jnp.int32, s_c.shape, 1)
    valid_c = (mi * dil + r) >= (wb + t - window)
    ti = lax.broadcasted_iota(jnp.int32, s_n.shape, 1)
    valid_n = (ti <= t) & (lax.rem(jnp.maximum(t - ti, 0), dil) == 0) & (ti < t_new)
    s_c = jnp.where(valid_c, s_c, NEG_INF)
    s_n = jnp.where(valid_n, s_n, NEG_INF)
    m = jnp.maximum(jnp.max(s_c, axis=1, keepdims=True), jnp.max(s_n, axis=1, keepdims=True))
    p_c = jnp.exp(s_c - m)
    p_n = jnp.exp(s_n - m)
    l = jnp.sum(p_c, axis=1, keepdims=True) + jnp.sum(p_n, axis=1, keepdims=True)
    o = (_dot(p_c, vc_ref[0], _NN, precise) + _dot(p_n, vn_ref[0], _NN, precise)) / l
    col_kv = lax.broadcasted_iota(jnp.int32, o.shape, 1) // HEAD_DIM
    row_kv = lax.broadcasted_iota(jnp.int32, o.shape, 0) // (n_heads // n_kv)
    o = jnp.where(col_kv == row_kv, o, 0.0)
    acc = o[:, 0:HEAD_DIM]
    for kh in range(1, n_kv):
        acc = acc + o[:, kh * HEAD_DIM:(kh + 1) * HEAD_DIM]
    o_ref[0, 0] = acc
    lse_ref[0, 0] = jnp.broadcast_to(m + jnp.log(l), lse_ref.shape[2:])


def gather_attention(q_exp, cache, k_new, v_new, g, t_new, precise):
    window, dil = WINDOWS[g], DILATIONS[g]
    batch, wb = cache.shape[0], cache.shape[1]
    n_kv = cache.shape[3]
    kvd = n_kv * HEAD_DIM
    assert wb % dil == 0
    rows = wb // dil
    n_heads = q_exp.shape[2]
    t_pad = k_new.shape[1]
    n_res = min(dil, t_new)
    cv = cache.reshape(batch, rows, dil, 2, n_kv, HEAD_DIM)[:, :, :n_res].reshape(batch, rows, n_res * 2 * kvd)
    o, lse = pl.pallas_call(
        functools.partial(_gather_attn_body, window=window, dil=dil, wb=wb, t_new=t_new, n_kv=n_kv,
                          precise=precise),
        grid=(batch, t_new),
        in_specs=[pl.BlockSpec((1, 1, n_heads, kvd), lambda b, t: (b, t, 0, 0)),
                  pl.BlockSpec((1, rows, kvd), lambda b, t: (b, 0, ((wb + t) % dil) * 2)),
                  pl.BlockSpec((1, rows, kvd), lambda b, t: (b, 0, ((wb + t) % dil) * 2 + 1)),
                  pl.BlockSpec((1, t_pad, kvd), lambda b, t: (b, 0, 0)),
                  pl.BlockSpec((1, t_pad, kvd), lambda b, t: (b, 0, 0))],
        out_specs=[pl.BlockSpec((1, 1, n_heads, HEAD_DIM), lambda b, t: (b, t, 0, 0)),
                   pl.BlockSpec((1, 1, n_heads, LANES), lambda b, t: (b, t, 0, 0))],
        out_shape=[jax.ShapeDtypeStruct((batch, t_new, n_heads, HEAD_DIM), F32),
                   jax.ShapeDtypeStruct((batch, t_new, n_heads, LANES), F32)],
        compiler_params=_cparams(("parallel", "arbitrary")),
        name="gather_attention",
    )(q_exp, cv, cv, k_new, v_new)
    lse = jnp.pad(lse[..., 0].reshape(batch * t_new, n_heads), ((0, 0), (0, LANES - n_heads)))
    return o.reshape(batch * t_new, n_heads * HEAD_DIM), lse


def _merge_body(o0_ref, o1_ref, o2_ref, l0_ref, l1_ref, l2_ref, e_ref, out_ref):
    l0, l1, l2 = l0_ref[...], l1_ref[...], l2_ref[...]
    mx = jnp.maximum(jnp.maximum(l0, l1), l2)
    e0, e1, e2 = jnp.exp(l0 - mx), jnp.exp(l1 - mx), jnp.exp(l2 - mx)
    inv = 1.0 / (e0 + e1 + e2)
    w_exps = []
    for e in (e0, e1, e2):
        wh, wl = _split_bf16(e * inv, 2)
        w_exps.append(jnp.dot(wh, e_ref[...], preferred_element_type=F32)
                      + jnp.dot(wl, e_ref[...], preferred_element_type=F32))
    for c in range(o0_ref.shape[0]):
        cs = slice(c * LANES, (c + 1) * LANES)
        acc = (w_exps[0][:, cs] * o0_ref[c] + w_exps[1][:, cs] * o1_ref[c] + w_exps[2][:, cs] * o2_ref[c])
        out_ref[:, cs] = acc.astype(out_ref.dtype)


def merge_groups(outs, lses, tm, out_dtype):
    planes, m, _ = outs[0].shape
    d_in = planes * LANES
    assert d_in // HEAD_DIM <= LANES
    expand = (jnp.arange(LANES)[:, None] == (jnp.arange(d_in) // HEAD_DIM)[None, :]).astype(BF16)
    o_spec = pl.BlockSpec((planes, tm, LANES), lambda i: (0, i, 0))
    l_spec = pl.BlockSpec((tm, LANES), lambda i: (i, 0))
    return pl.pallas_call(
        _merge_body,
        grid=(m // tm,),
        in_specs=[o_spec, o_spec, o_spec, l_spec, l_spec, l_spec,
                  pl.BlockSpec((LANES, d_in), lambda i: (0, 0))],
        out_specs=pl.BlockSpec((tm, d_in), lambda i: (i, 0)),
        out_shape=jax.ShapeDtypeStruct((m, d_in), out_dtype),
        compiler_params=_cparams(("parallel",)),
        name="merge_groups",
    )(*outs, *lses, expand)


def _swiglu_rows(h_ref, wg, wu, o_ref, precise):
    for rows in _row_chunks(h_ref.shape[0]):
        h = h_ref[rows, :]
        g = _dot(h, wg[...], precise=precise)
        u = _dot(h, wu[...], precise=precise)
        o_ref[rows, :] = (g * _sigmoid(g) * u).astype(o_ref.dtype)


def _gateup_body(h_ref, wg_ref, wu_ref, o_ref, wgbf_ref, wubf_ref, *, precise):
    _swiglu_rows(h_ref, _weight_tile(wg_ref, wgbf_ref, precise), _weight_tile(wu_ref, wubf_ref, precise),
                 o_ref, precise)


def gate_up(h, w_gate, w_up, layer, tm, tn=512, precise=False):
    m, d = h.shape
    f = w_gate.shape[2]
    w_spec = pl.BlockSpec((None, d, tn), lambda j, i: (layer, 0, j))
    return pl.pallas_call(
        functools.partial(_gateup_body, precise=precise),
        grid=(f // tn, m // tm),
        in_specs=[pl.BlockSpec((tm, d), lambda j, i: (i, 0)), w_spec, w_spec],
        out_specs=pl.BlockSpec((tm, tn), lambda j, i: (i, j)),
        out_shape=jax.ShapeDtypeStruct((m, f), F32 if precise else BF16),
        scratch_shapes=[_weight_scratch(d, tn, precise), _weight_scratch(d, tn, precise)],
        compiler_params=_cparams(("parallel", "arbitrary")),
        name="gate_up",
    )(h, w_gate, w_up)


def _gather_rows_body(tok_ref, src_hbm, src2_hbm, o_ref, buf, sem, *, tm, split):
    i = pl.program_id(0)
    n = pl.num_programs(0)

    def issue(step, slot):
        def body(r, carry):
            tok = tok_ref[step * tm + r]
            dst = buf.at[slot, pl.ds(r, 1)]

            @pl.when(tok < split)
            def _():
                pltpu.make_async_copy(src_hbm.at[pl.ds(tok, 1)], dst, sem.at[slot]).start()

            @pl.when(tok >= split)
            def _():
                pltpu.make_async_copy(src2_hbm.at[pl.ds(tok - split, 1)], dst, sem.at[slot]).start()

            return carry
        lax.fori_loop(0, tm, body, 0)

    @pl.when(i == 0)
    def _():
        issue(0, 0)

    slot = i % 2

    @pl.when(i + 1 < n)
    def _():
        issue(i + 1, 1 - slot)

    pltpu.make_async_copy(src_hbm.at[pl.ds(0, tm)], buf.at[slot], sem.at[slot]).wait()
    o_ref[...] = buf[slot].astype(o_ref.dtype)


def gather_rows(tok, src, src2, tm, out_dtype):
    n_slots = tok.shape[0]
    rows, d = src.shape
    assert n_slots % tm == 0 and rows >= tm and src2.shape[1] == d and src2.dtype == src.dtype
    grid_spec = pltpu.PrefetchScalarGridSpec(
        num_scalar_prefetch=1,
        grid=(n_slots // tm,),
        in_specs=[pl.BlockSpec(memory_space=pl.ANY), pl.BlockSpec(memory_space=pl.ANY)],
        out_specs=pl.BlockSpec((tm, d), lambda i, tok: (i, 0)),
        scratch_shapes=[pltpu.VMEM((2, tm, d), src.dtype), pltpu.SemaphoreType.DMA((2,))])
    return pl.pallas_call(
        functools.partial(_gather_rows_body, tm=tm, split=rows),
        grid_spec=grid_spec,
        out_shape=jax.ShapeDtypeStruct((n_slots, d), out_dtype),
        compiler_params=_cparams(("arbitrary",)),
        name="gather_rows",
    )(tok, src, src2)


def _expert_changed(te_ref):
    i = pl.program_id(1)
    return (i == 0) | (te_ref[i] != te_ref[jnp.maximum(i - 1, 0)])


def _moe_gateup_body(te_ref, na_ref, x_ref, wg_ref, wu_ref, o_ref, wgbf_ref, wubf_ref):
    i = pl.program_id(1)

    @pl.when(_expert_changed(te_ref))
    def _():
        wgbf_ref[...] = wg_ref[...].astype(BF16)
        wubf_ref[...] = wu_ref[...].astype(BF16)

    @pl.when(i < na_ref[0])
    def _():
        _swiglu_rows(x_ref, wgbf_ref, wubf_ref, o_ref, False)

    @pl.when(i >= na_ref[0])
    def _():
        o_ref[...] = jnp.zeros_like(o_ref)


def moe_gate_up(tile_expert, n_active, xs, w_gate, w_up, layer, tm, tn=512):
    npad, d = xs.shape
    f = w_gate.shape[3]
    w_spec = pl.BlockSpec((None, None, d, tn), lambda j, i, te, na: (layer, te[i], 0, j))
    grid_spec = pltpu.PrefetchScalarGridSpec(
        num_scalar_prefetch=2,
        grid=(f // tn, npad // tm),
        in_specs=[pl.BlockSpec((tm, d), lambda j, i, te, na: (i, 0)), w_spec, w_spec],
        out_specs=pl.BlockSpec((tm, tn), lambda j, i, te, na: (i, j)),
        scratch_shapes=[pltpu.VMEM((d, tn), BF16), pltpu.VMEM((d, tn), BF16)])
    return pl.pallas_call(
        _moe_gateup_body,
        grid_spec=grid_spec,
        out_shape=jax.ShapeDtypeStruct((npad, f), BF16),
        compiler_params=_cparams(("arbitrary", "arbitrary")),
        name="moe_gate_up",
    )(tile_expert, n_active, xs, w_gate, w_up)


def _moe_down_body(te_ref, na_ref, a_ref, w_ref, gate_ref, o_ref, wbf_ref):
    i = pl.program_id(1)

    @pl.when(_expert_changed(te_ref))
    def _():
        wbf_ref[...] = w_ref[...].astype(BF16)

    @pl.when(i < na_ref[0])
    def _():
        for rows in _row_chunks(a_ref.shape[0]):
            o_ref[rows, :] = (jnp.dot(a_ref[rows, :], wbf_ref[...], preferred_element_type=F32)
                              * gate_ref[rows, :])

    @pl.when(i >= na_ref[0])
    def _():
        o_ref[...] = jnp.zeros_like(o_ref)


def moe_down(tile_expert, n_active, a, w_down, slot_gate, layer, tm, tn=512):
    npad, f = a.shape
    d = w_down.shape[3]
    grid_spec = pltpu.PrefetchScalarGridSpec(
        num_scalar_prefetch=2,
        grid=(d // tn, npad // tm),
        in_specs=[pl.BlockSpec((tm, f), lambda j, i, te, na: (i, 0)),
                  pl.BlockSpec((None, None, f, tn), lambda j, i, te, na: (layer, te[i], 0, j)),
                  pl.BlockSpec((tm, 1), lambda j, i, te, na: (i, 0))],
        out_specs=pl.BlockSpec((tm, tn), lambda j, i, te, na: (i, j)),
        scratch_shapes=[pltpu.VMEM((f, tn), BF16)])
    return pl.pallas_call(
        _moe_down_body,
        grid_spec=grid_spec,
        out_shape=jax.ShapeDtypeStruct((npad, d), F32),
        compiler_params=_cparams(("arbitrary", "arbitrary")),
        name="moe_down",
    )(tile_expert, n_active, a, w_down, slot_gate)


def moe_plan(idx, gates, tm, n_tiles):
    n_tok = idx.shape[0]
    e_flat = idx.reshape(-1)
    order = jnp.argsort(e_flat, stable=True).astype(jnp.int32)
    onehot = (e_flat[:, None] == jnp.arange(N_EXPERTS)[None, :]).astype(jnp.int32)
    counts = jnp.sum(onehot, axis=0)
    within_expert = jnp.sum((jnp.cumsum(onehot, axis=0) - onehot) * onehot, axis=1)
    tiles_per = (counts + tm - 1) // tm
    tile_end = jnp.cumsum(tiles_per)
    n_active = tile_end[-1]
    slot_start = (tile_end - tiles_per) * tm
    sorted_start = jnp.cumsum(counts) - counts
    tile_ids = jnp.minimum(jnp.arange(n_tiles), n_active - 1)
    tile_expert = jnp.sum((tile_ids[:, None] >= tile_end[None, :]).astype(jnp.int32), axis=1)
    slot = jnp.arange(n_tiles * tm)
    slot_e = jnp.repeat(tile_expert, tm)
    within = slot - slot_start[slot_e]
    slot_ok = (within < counts[slot_e]) & (slot < n_active * tm)
    src = order[jnp.clip(sorted_start[slot_e] + within, 0, 2 * n_tok - 1)]
    slot_token = jnp.where(slot_ok, src // 2, 0)
    slot_gate = jnp.where(slot_ok, gates.reshape(-1)[src], 0.0)
    pos = slot_start[e_flat] + within_expert
    return (tile_expert.astype(jnp.int32), n_active.reshape(1).astype(jnp.int32),
            slot_token, slot_gate.reshape(-1, 1).astype(F32), pos.reshape(n_tok, 2))


def _softplus(x):
    return jnp.maximum(x, 0.0) + jnp.log(1.0 + jnp.exp(-jnp.abs(x)))


def _ssd_body(x_ref, bm_ref, cm_ref, z_ref, dt_ref, dtt_ref,
              cwx_ref, cwb_ref, cwc_ref, cbx_ref, cbb_ref, cbc_ref,
              dtb_ref, dtbt_ref, al_ref, alt_ref, dsk_ref, nw_ref,
              cix_ref, cib_ref, cic_ref, h0_ref, eexp_ref, eall_ref,
              g_ref, hf_ref,
              xbuf, bbuf, cbuf, state, *, q, valid, hpg, precise):
    c = pl.program_id(2)
    nc = pl.num_programs(2)
    halo = SUBLANES

    @pl.when(c == 0)
    def _():
        xbuf[0:halo] = cix_ref[0]
        bbuf[0:halo] = cib_ref[0]
        cbuf[0:halo] = cic_ref[0]
        state[...] = h0_ref[0].T

    xbuf[halo:halo + q] = x_ref[0]
    bbuf[halo:halo + q] = bm_ref[0]
    cbuf[halo:halo + q] = cm_ref[0]

    def conv_silu(buf, cw_ref, cb_ref):
        acc = cb_ref[...]
        for j in range(CONV_W):
            lo = halo - (CONV_W - 1) + j
            acc = acc + buf[lo:lo + q] * cw_ref[j:j + 1]
        return acc * _sigmoid(acc)

    xs = conv_silu(xbuf, cwx_ref, cbx_ref)
    bmat = conv_silu(bbuf, cwb_ref, cbb_ref)
    cmat = conv_silu(cbuf, cwc_ref, cbc_ref)
    xbuf[0:halo] = xbuf[q:q + halo]
    bbuf[0:halo] = bbuf[q:q + halo]
    cbuf[0:halo] = cbuf[q:q + halo]

    dt = _softplus(dt_ref[0] + dtb_ref[...])
    dtt = _softplus(dtt_ref[0] + dtbt_ref[...])
    if valid < q:
        dt = jnp.where(lax.broadcasted_iota(jnp.int32, dt.shape, 0) < valid, dt, 0.0)
        dtt = jnp.where(lax.broadcasted_iota(jnp.int32, dtt.shape, 1) < valid, dtt, 0.0)
    a = dt * (-jnp.exp(al_ref[...]))
    at = dtt * (-jnp.exp(alt_ref[...]))
    ii = lax.broadcasted_iota(jnp.int32, (q, q), 0)
    jj = lax.broadcasted_iota(jnp.int32, (q, q), 1)
    lower = ii >= jj
    tril = lower.astype(BF16)
    triu = (ii <= jj).astype(BF16)
    a_cum = sum(jnp.dot(tril, part, preferred_element_type=F32) for part in _split_bf16(a, 3))
    a_cumt = sum(jnp.dot(part, triu, preferred_element_type=F32) for part in _split_bf16(at, 3))
    a_last = a_cum[q - 1:q, :]
    e_cum = jnp.exp(a_cum)
    e_end = jnp.exp(a_last - a_cum)

    expand = lambda v, e_ref: sum(jnp.dot(part, e_ref[...], preferred_element_type=F32)
                                  for part in _split_bf16(v, 3))
    per_lane = expand(jnp.concatenate([dt, e_cum, e_end], axis=0), eexp_ref)
    dt_x, ecum_x, eend_x = per_lane[0:q], per_lane[q:2 * q], per_lane[2 * q:3 * q]
    col_all = expand(a_cum, eall_ref)

    xc = xs * dt_x
    xw = xc * eend_x
    if not precise:
        bmat = bmat.astype(BF16)
        cmat = cmat.astype(BF16)
        xc = xc.astype(BF16)
    st = state[...]
    cb = _dot(cmat, bmat, _NT, precise)
    y = _dot(cmat, st, _NN, precise) * ecum_x + dsk_ref[...] * xs

    lo = lax.broadcasted_iota(jnp.int32, (q, LANES), 1) < SSM_HEAD_DIM
    pairs = []
    for u in range(hpg // 2):
        xcp = xc[:, u * LANES:(u + 1) * LANES]
        y2 = []
        for h in (2 * u, 2 * u + 1):
            seg = col_all[:, h * q:(h + 1) * q] - a_cumt[h:h + 1, :]
            y2.append(_dot(cb * jnp.exp(jnp.where(lower, seg, NEG_INF)), xcp, _NN, precise))
        pairs.append(jnp.where(lo, y2[0], y2[1]))
    y = y + jnp.concatenate(pairs, axis=1)
    state[...] = st * ecum_x[q - 1:q, :] + _dot(bmat, xw, _TN, precise)

    z = z_ref[0]
    gz = y * (z * _sigmoid(z))
    ms = jnp.mean(gz * gz, axis=1, keepdims=True)
    g_ref[0] = (gz * lax.rsqrt(ms + EPS) * nw_ref[...]).astype(g_ref.dtype)

    @pl.when(c == nc - 1)
    def _():
        hf_ref[0] = state[...].T


def ssd_core(z, xbc, dt_raw, conv_init, h0, conv_w, conv_b, dt_bias, a_log, d_skip, norm_w, *, batch, seq, q, valid,
             precise):
    d_inner = z.shape[1]
    n_heads = dt_raw.shape[1]
    hpg = n_heads // SSM_GROUPS
    gw = hpg * SSM_HEAD_DIM
    nst = D_STATE
    assert seq % q == 0 and gw % LANES == 0
    nc = seq // q
    xb_blk = d_inner // nst
    cm_blk = xb_blk + SSM_GROUPS
    z3 = z.reshape(batch, seq, d_inner)
    xbc3 = xbc.reshape(batch, seq, xbc.shape[1])
    hp2 = 2 * hpg
    assert q == LANES and 2 * SSM_HEAD_DIM == LANES
    dt4 = jnp.pad(dt_raw.reshape(batch, seq, SSM_GROUPS, hpg).transpose(0, 2, 1, 3),
                  ((0, 0), (0, 0), (0, 0), (0, hpg)))
    dtt4 = dt4.transpose(0, 1, 3, 2)
    pad_cols = lambda v: jnp.pad(v.reshape(SSM_GROUPS, 1, hpg), ((0, 0), (0, 0), (0, hpg)))
    pad_rows = lambda v: jnp.pad(v.reshape(SSM_GROUPS, hpg, 1), ((0, 0), (0, hpg), (0, 0)))
    dsk = jnp.repeat(d_skip, SSM_HEAD_DIM).reshape(1, d_inner)
    cb2 = conv_b.reshape(1, -1)
    nw = norm_w.reshape(1, d_inner)
    head = jnp.arange(hp2)[:, None]
    e_exp = (head == (jnp.arange(gw) // SSM_HEAD_DIM)[None, :]).astype(BF16)
    e_all = (head == (jnp.arange(hpg * q) // q)[None, :]).astype(BF16)

    grp = lambda b, g, c: (b, c, g)
    in_specs = [
        pl.BlockSpec((1, q, gw), grp),
        pl.BlockSpec((1, q, nst), lambda b, g, c: (b, c, xb_blk + g)),
        pl.BlockSpec((1, q, nst), lambda b, g, c: (b, c, cm_blk + g)),
        pl.BlockSpec((1, q, gw), grp),
        pl.BlockSpec((1, None, q, hp2), lambda b, g, c: (b, g, c, 0)),
        pl.BlockSpec((1, None, hp2, q), lambda b, g, c: (b, g, 0, c)),
        pl.BlockSpec((CONV_W, gw), lambda b, g, c: (0, g)),
        pl.BlockSpec((CONV_W, nst), lambda b, g, c: (0, xb_blk + g)),
        pl.BlockSpec((CONV_W, nst), lambda b, g, c: (0, cm_blk + g)),
        pl.BlockSpec((1, gw), lambda b, g, c: (0, g)),
        pl.BlockSpec((1, nst), lambda b, g, c: (0, xb_blk + g)),
        pl.BlockSpec((1, nst), lambda b, g, c: (0, cm_blk + g)),
        pl.BlockSpec((None, 1, hp2), lambda b, g, c: (g, 0, 0)),
        pl.BlockSpec((None, hp2, 1), lambda b, g, c: (g, 0, 0)),
        pl.BlockSpec((None, 1, hp2), lambda b, g, c: (g, 0, 0)),
        pl.BlockSpec((None, hp2, 1), lambda b, g, c: (g, 0, 0)),
        pl.BlockSpec((1, gw), lambda b, g, c: (0, g)),
        pl.BlockSpec((1, gw), lambda b, g, c: (0, g)),
        pl.BlockSpec((1, SUBLANES, gw), lambda b, g, c: (b, 0, g)),
        pl.BlockSpec((1, SUBLANES, nst), lambda b, g, c: (b, 0, xb_blk + g)),
        pl.BlockSpec((1, SUBLANES, nst), lambda b, g, c: (b, 0, cm_blk + g)),
        pl.BlockSpec((1, gw, nst), lambda b, g, c: (b, g, 0)),
        pl.BlockSpec((hp2, gw), lambda b, g, c: (0, 0)),
        pl.BlockSpec((hp2, hpg * q), lambda b, g, c: (0, 0)),
    ]
    g_out, h_fin = pl.pallas_call(
        functools.partial(_ssd_body, q=q, valid=valid, hpg=hpg, precise=precise),
        grid=(batch, SSM_GROUPS, nc),
        in_specs=in_specs,
        out_specs=[pl.BlockSpec((1, q, gw), grp),
                   pl.BlockSpec((1, gw, nst), lambda b, g, c: (b, g, 0))],
        out_shape=[jax.ShapeDtypeStruct((batch, seq, d_inner), F32 if precise else BF16),
                   jax.ShapeDtypeStruct(h0.shape, F32)],
        scratch_shapes=[pltpu.VMEM((q + SUBLANES, gw), F32),
                        pltpu.VMEM((q + SUBLANES, nst), F32),
                        pltpu.VMEM((q + SUBLANES, nst), F32),
                        pltpu.VMEM((nst, gw), F32)],
        compiler_params=_cparams(("parallel", "parallel", "arbitrary")),
        name="ssd_core",
    )(xbc3, xbc3, xbc3, z3, dt4, dtt4,
      conv_w, conv_w, conv_w, cb2, cb2, cb2,
      pad_cols(dt_bias), pad_rows(dt_bias), pad_cols(a_log), pad_rows(a_log), dsk, nw,
      conv_init, conv_init, conv_init, h0, e_exp, e_all)
    return g_out.reshape(batch * seq, d_inner), h_fin


def _attn_tokens(h, w_qkv, layer, q_gain, k_gain, tables, tm, table_period_rows, precise):
    q_cols = (w_qkv.shape[2] * KV_RATIO) // (KV_RATIO + 2)
    kv_cols = (w_qkv.shape[2] - q_cols) // 2
    q = proj_normrope(h, w_qkv, layer, q_gain, tables, col_off=0, n=q_cols, n_rope_cols=q_cols,
                      tm=tm, table_period_rows=table_period_rows, precise=precise)
    kv = proj_normrope(h, w_qkv, layer, k_gain, tables, col_off=q_cols, n=2 * kv_cols, n_rope_cols=kv_cols,
                       tm=tm, table_period_rows=table_period_rows, precise=precise)
    return q, kv


def _kv_window(kv, g, batch, seq, keep):
    per = kv.shape[0] // (2 * N_GROUPS)
    kv4 = kv.reshape(2 * N_GROUPS, per, batch, seq, LANES)[:, :, :, seq - keep:]
    kv_g = jnp.stack([kv4[g], kv4[N_GROUPS + g]], axis=0)
    return kv_g.transpose(2, 3, 0, 1, 4).reshape(batch, keep, 2, per * LANES // HEAD_DIM, HEAD_DIM)


def _planes_to_rows(x):
    return x.transpose(1, 0, 2).reshape(x.shape[1], x.shape[0] * LANES)


def _rows_to_planes(x):
    return x.reshape(x.shape[0], x.shape[1] // LANES, LANES).transpose(1, 0, 2)


def kernel(x_prompt, x_sample, cache_kv_w128, cache_kv_w512, cache_kv_w2048, state_conv, state_ssm,
           norm_mix, norm_ffn, attn_w_qkv, attn_q_norm, attn_k_norm, attn_w_o,
           ssm_w_in, ssm_conv_w, ssm_conv_b, ssm_dt_bias, ssm_a_log, ssm_d, ssm_norm, ssm_w_out,
           ffn_w_gate, ffn_w_up, ffn_w_down, moe_w_router, moe_w_gate, moe_w_up, moe_w_down):
    bp, sp, d_model = x_prompt.shape
    bs, ts, _ = x_sample.shape
    depth = norm_mix.shape[0]
    mp, ms = bp * sp, bs * ts
    past_len_caches = (cache_kv_w128, cache_kv_w512, cache_kv_w2048)
    tmp = 1024
    tmp_deep = 512
    tms = ms
    tm_moe = 512
    tm_moe_down = 256
    n_heads = attn_w_o.shape[1] // HEAD_DIM
    n_kv = n_heads // KV_RATIO
    kvd = n_kv * HEAD_DIM
    d_inner = ssm_w_out.shape[1]
    ssm_heads = ssm_dt_bias.shape[1]
    conv_dim = ssm_conv_w.shape[2]
    t_pad = 16

    xp = x_prompt.reshape(mp, d_model)
    xs = x_sample.reshape(ms, d_model)

    tab_p = rope_tables(jnp.arange(sp))
    tab_s = rope_tables(PAST_LEN + (jnp.arange(ms) % ts))

    kv_p = [[] for _ in range(N_GROUPS)]
    kv_s = [[] for _ in range(N_GROUPS)]
    conv_p, conv_s, ssm_p, ssm_s = [], [], [], []

    for i in range(depth):
        j = i // 2
        hp = rmsnorm(xp, norm_mix[i], tmp)
        hs = rmsnorm(xs, norm_mix[i], tms, out_dtype=F32)
        if i % 2 == 0:
            q, kv = _attn_tokens(hp, attn_w_qkv, j, attn_q_norm[j], attn_k_norm[j], tab_p, tmp, sp, False)
            outs, lses = [], []
            for g in range(N_GROUPS):
                o, l = band_attention(q, kv, g, bp, sp)
                outs.append(o)
                lses.append(l)
                kv_p[g].append(_kv_window(kv, g, bp, sp, min(WINDOWS[g], sp)))
            xp = matmul(merge_groups(outs, lses, 256, BF16), attn_w_o, j, residual=xp, tm=tmp, tn=512)
            q, kv = _attn_tokens(hs, attn_w_qkv, j, attn_q_norm[j], attn_k_norm[j], tab_s, tms, ms, True)
            q6 = _planes_to_rows(q).reshape(bs, ts, N_GROUPS, n_kv, KV_RATIO, HEAD_DIM)
            kv5 = _planes_to_rows(kv).reshape(bs, ts, 2 * N_GROUPS, kvd)
            eye = jnp.eye(n_kv, dtype=q.dtype)
            outs, lses = [], []
            for g in range(N_GROUPS):
                q_exp = (q6[:, :, g, :, :, None, :] * eye[None, None, :, None, :, None]).reshape(
                    bs, ts, n_heads, kvd)
                k_new = jnp.pad(kv5[:, :, g], ((0, 0), (0, t_pad - ts), (0, 0)))
                v_new = jnp.pad(kv5[:, :, N_GROUPS + g], ((0, 0), (0, t_pad - ts), (0, 0)))
                cache = past_len_caches[g][j]
                o, l = gather_attention(q_exp, cache, k_new, v_new, g, ts, True)
                outs.append(_rows_to_planes(o))
                lses.append(l)
                kv_s[g].append(jnp.stack([kv5[:, :, g], kv5[:, :, N_GROUPS + g]], axis=2).reshape(
                    bs, ts, 2, n_kv, HEAD_DIM))
            xs = matmul(merge_groups(outs, lses, tms, F32), attn_w_o, j, residual=xs, tm=tms, tn=512,
                        precise=True)
            hp = rmsnorm(xp, norm_ffn[i], tmp)
            hs = rmsnorm(xs, norm_ffn[i], tms, out_dtype=F32)
            act = gate_up(hp, ffn_w_gate, ffn_w_up, j, tmp)
            k_half = act.shape[1] // 2
            for k_off in (0, k_half):
                xp = matmul(act, ffn_w_down, j, k_off=k_off, kdim=k_half, residual=xp, tm=tmp, tn=512)
            xs = matmul(gate_up(hs, ffn_w_gate, ffn_w_up, j, tms, precise=True), ffn_w_down, j,
                        residual=xs, tm=tms, tn=256, precise=True)
        else:
            w_dt = ssm_w_in[j:j + 1, :, d_inner + conv_dim:]

            def mixer(h, x, tm, batch, seq_pad, seq, conv_init, h0, precise):
                z = matmul(h, ssm_w_in, j, n=d_inner, col_off=0, tm=tm, tn=512, precise=precise)
                xbc = matmul(h, ssm_w_in, j, n=conv_dim, col_off=d_inner, tm=tm, tn=512, precise=precise)
                dt_raw = matmul(h, w_dt, 0, tm=tm, tn=ssm_heads, precise=precise)
                if seq_pad != seq:
                    padt = lambda v: jnp.pad(v.reshape(batch, seq, -1),
                                             ((0, 0), (0, seq_pad - seq), (0, 0))).reshape(batch * seq_pad, -1)
                    zk, xbck, dtk = padt(z), padt(xbc), padt(dt_raw)
                else:
                    zk, xbck, dtk = z, xbc, dt_raw
                y, h_fin = ssd_core(zk, xbck, dtk, conv_init, h0, ssm_conv_w[j], ssm_conv_b[j], ssm_dt_bias[j],
                                    ssm_a_log[j], ssm_d[j], ssm_norm[j], batch=batch, seq=seq_pad,
                                    q=SSM_CHUNK, valid=min(seq, SSM_CHUNK), precise=precise)
                if seq_pad != seq:
                    y = y.reshape(batch, seq_pad, d_inner)[:, :seq].reshape(batch * seq, d_inner)
                x_new = matmul(y, ssm_w_out, j, residual=x, tm=tm, tn=512, precise=precise)
                return x_new, xbc.reshape(batch, seq, conv_dim), h_fin

            zero_conv = jnp.zeros((bp, SUBLANES, conv_dim), F32)
            zero_h = jnp.zeros((bp, ssm_heads * SSM_HEAD_DIM, D_STATE), F32)
            xp, xbc_p, hfin_p = mixer(hp, xp, tmp, bp, sp, sp, zero_conv, zero_h, False)
            conv_p.append(xbc_p[:, sp - (CONV_W - 1):])
            ssm_p.append(hfin_p.reshape(bp, ssm_heads, SSM_HEAD_DIM, D_STATE))

            prev = state_conv[j]
            conv_init_s = jnp.pad(prev, ((0, 0), (SUBLANES - (CONV_W - 1), 0), (0, 0)))
            h0_s = state_ssm[j].reshape(bs, ssm_heads * SSM_HEAD_DIM, D_STATE)
            xs, xbc_s, hfin_s = mixer(hs, xs, tms, bs, SSM_CHUNK, ts, conv_init_s, h0_s, True)
            conv_s.append(jnp.concatenate([prev, xbc_s], axis=1)[:, -(CONV_W - 1):])
            ssm_s.append(hfin_s.reshape(bs, ssm_heads, SSM_HEAD_DIM, D_STATE))

            hp, gate_p, idx_p = rmsnorm_router(xp, norm_ffn[i], moe_w_router[j], tmp_deep)
            hs, gate_s, idx_s = rmsnorm_router(xs, norm_ffn[i], moe_w_router[j], tms)
            gates = jnp.concatenate([gate_p[:, :2], gate_s[:, :2]], axis=0)
            idx = jnp.concatenate([idx_p[:, :2], idx_s[:, :2]], axis=0)
            n_tok = mp + ms
            n_tiles = (2 * n_tok + N_EXPERTS * (tm_moe - 1)) // tm_moe
            tile_expert, n_active, slot_token, slot_gate, pos = moe_plan(idx, gates, tm_moe, n_tiles)
            x_sorted = gather_rows(slot_token, hp, hs, tm_moe, BF16)
            act = moe_gate_up(tile_expert, n_active, x_sorted, moe_w_gate, moe_w_up, j, tm_moe, tn=1024)
            halves = tm_moe // tm_moe_down
            y_sorted = moe_down(jnp.repeat(tile_expert, halves), n_active * halves, act, moe_w_down, slot_gate,
                                j, tm_moe_down, tn=512)
            xp = xp + jnp.take(y_sorted, pos[:mp, 0], axis=0) + jnp.take(y_sorted, pos[:mp, 1], axis=0)
            xs = xs + jnp.take(y_sorted, pos[mp:, 0], axis=0) + jnp.take(y_sorted, pos[mp:, 1], axis=0)

    stack = lambda xs_: jnp.stack(xs_, 0)

    def rolled(cache, new_rows):
        wb = cache.shape[2]
        return jnp.concatenate([cache, stack(new_rows)], axis=2)[:, :, -wb:]

    return (xp.reshape(bp, sp, d_model), xs.reshape(bs, ts, d_model),
            stack(kv_p[0]), stack(kv_p[1]), stack(kv_p[2]), stack(conv_p), stack(ssm_p),
            rolled(cache_kv_w128, kv_s[0]), rolled(cache_kv_w512, kv_s[1]), rolled(cache_kv_w2048, kv_s[2]),
            stack(conv_s), stack(ssm_s))
```
